```python
import jax, jax.numpy as jnp
from jax import lax
import numpy as np

D_MODEL = 1024
BATCH = 16
SEQ = 2048
DEPTH = 1

HEAD_DIM = 64
N_Q_HEADS = 8
N_KV_HEADS = 2
GROUP = N_Q_HEADS // N_KV_HEADS
ATTN_WIDTH = N_Q_HEADS * HEAD_DIM
KV_WIDTH = N_KV_HEADS * HEAD_DIM
WINDOW = 128
BLOCK = 128
ROPE_THETA = 10000.0
LRU_WIDTH = D_MODEL // 2
LRU_HEADS = 8
LRU_BLOCK = LRU_WIDTH // LRU_HEADS
LRU_C = 8.0
CONV_WIDTH = 4
N_EXPERTS = 32
TOP_K = 4
D_EXPERT = D_MODEL
SWIGLU_LIMIT = 7.0
SWIGLU_ALPHA = 1.702
MOE_CHUNK = 128
NORM_EPS = 1e-6
NEG_INF = -1e30
IN_SPLITS = (ATTN_WIDTH, KV_WIDTH, KV_WIDTH, LRU_WIDTH, LRU_WIDTH, D_MODEL, D_MODEL)
IN_WIDTH = sum(IN_SPLITS)

kernel_name = "hybrid_swa_sink_rglru_moe_block"


def rmsnorm(x, g):
    x32 = x.astype(jnp.float32)
    y = x32 * lax.rsqrt(jnp.mean(x32 * x32, axis=-1, keepdims=True) + NORM_EPS)
    return (y * g.astype(jnp.float32)).astype(x.dtype)


def split_cols(p):
    outs, o = [], 0
    for w in IN_SPLITS:
        outs.append(p[..., o:o + w])
        o += w
    return outs


def rope(t, positions):
    hd = t.shape[-1]
    inv_freq = 1.0 / (ROPE_THETA ** (jnp.arange(0, hd, 2, dtype=jnp.float32) / hd))
    ang = positions[..., None].astype(jnp.float32) * inv_freq
    cos = jnp.cos(ang)[:, :, None, :]
    sin = jnp.sin(ang)[:, :, None, :]
    t32 = t.astype(jnp.float32)
    t1, t2 = t32[..., :hd // 2], t32[..., hd // 2:]
    out = jnp.concatenate([t1 * cos - t2 * sin, t2 * cos + t1 * sin], axis=-1)
    return out.astype(t.dtype)


def sliding_window_attention(q, k, v, sinks):
    bsz, s = q.shape[0], q.shape[1]
    nb = s // BLOCK
    qb = q.reshape(bsz, nb, BLOCK, N_KV_HEADS, GROUP, HEAD_DIM)

    def with_prev(t):
        tb = t.reshape(bsz, nb, BLOCK, N_KV_HEADS, HEAD_DIM)
        prev = jnp.concatenate([jnp.zeros_like(tb[:, :1]), tb[:, :-1]], axis=1)
        return jnp.concatenate([prev, tb], axis=2)

    kk, vv = with_prev(k), with_prev(v)
    scores = jnp.einsum('bnqhgd,bnkhd->bhgnqk', qb, kk).astype(jnp.float32) * (HEAD_DIM ** -0.5)
    blk = jnp.arange(nb)[:, None, None]
    qpos = blk * BLOCK + jnp.arange(BLOCK)[None, :, None]
    kpos = (blk - 1) * BLOCK + jnp.arange(2 * BLOCK)[None, None, :]
    diff = qpos - kpos
    mask = (diff >= 0) & (diff < WINDOW) & (kpos >= 0)
    scores = jnp.where(mask, scores, NEG_INF)
    sink = sinks.astype(jnp.float32).reshape(N_KV_HEADS, GROUP)[None, :, :, None, None, None]
    m = jnp.maximum(jnp.max(scores, axis=-1, keepdims=True), sink)
    p = jnp.exp(scores - m)
    p = p / (jnp.sum(p, axis=-1, keepdims=True) + jnp.exp(sink - m))
    o = jnp.einsum('bhgnqk,bnkhd->bnqhgd', p.astype(v.dtype), vv)
    return o.reshape(bsz, s, ATTN_WIDTH)


def causal_depthwise_conv(xr, w, b):
    y = lax.conv_general_dilated(xr, w[:, None, :].astype(xr.dtype), window_strides=(1,),
                                 padding=[(CONV_WIDTH - 1, 0)],
                                 dimension_numbers=('NWC', 'WIO', 'NWC'),
                                 feature_group_count=LRU_WIDTH)
    return y + b


def rg_lru(xb, w_rg, b_rg, w_ig, b_ig, lam):
    bsz, s, _ = xb.shape
    x32 = xb.astype(jnp.float32)
    xh = x32.reshape(bsz, s, LRU_HEADS, LRU_BLOCK)
    r = jax.nn.sigmoid(jnp.einsum('bshi,hij->bshj', xh, w_rg.astype(jnp.float32)).reshape(bsz, s, LRU_WIDTH) + b_rg.astype(jnp.float32))
    i = jax.nn.sigmoid(jnp.einsum('bshi,hij->bshj', xh, w_ig.astype(jnp.float32)).reshape(bsz, s, LRU_WIDTH) + b_ig.astype(jnp.float32))
    log_a = -LRU_C * r * jax.nn.softplus(-lam.astype(jnp.float32))
    a = jnp.exp(log_a)
    u = jnp.sqrt(-jnp.expm1(2.0 * log_a)) * (i * x32)

    def combine(c1, c2):
        a1, b1 = c1
        a2, b2 = c2
        return a1 * a2, a2 * b1 + b2

    _, h = lax.associative_scan(combine, (a, u), axis=1)
    return h.astype(xb.dtype)


def mixer(h, positions, w_in, conv_w, conv_b, w_rg, b_rg, w_ig, b_ig, lru_lambda,
          attn_sinks, w_attn_o, w_lru_o, w_out):
    bsz, s, _ = h.shape
    q, k, v, xr, yr, ga, gl = split_cols(h @ w_in)
    q = rope(q.reshape(bsz, s, N_Q_HEADS, HEAD_DIM), positions)
    k = rope(k.reshape(bsz, s, N_KV_HEADS, HEAD_DIM), positions)
    v = v.reshape(bsz, s, N_KV_HEADS, HEAD_DIM)
    y_attn = sliding_window_attention(q, k, v, attn_sinks) @ w_attn_o
    xr = causal_depthwise_conv(xr, conv_w, conv_b)
    y_lru = (rg_lru(xr, w_rg, b_rg, w_ig, b_ig, lru_lambda) * jax.nn.gelu(yr)) @ w_lru_o
    merged = jax.nn.sigmoid(ga) * y_attn + jax.nn.sigmoid(gl) * y_lru
    return merged @ w_out


def moe(h, w_router, b_router, w_gate_up, b_gate_up, w_down, b_down):
    bsz, s, d = h.shape
    n = bsz * s
    t = h.reshape(n, d)
    logits = (t @ w_router + b_router).astype(jnp.float32)
    top_val, top_idx = lax.top_k(logits, TOP_K)
    gate = jax.nn.softmax(top_val, axis=-1)
    nk = n * TOP_K
    flat_e = top_idx.reshape(-1).astype(jnp.int32)
    flat_tok = jnp.repeat(jnp.arange(n, dtype=jnp.int32), TOP_K)
    flat_w = gate.reshape(-1)
    order = jnp.argsort(flat_e)
    se = flat_e[order]
    counts = jnp.zeros((N_EXPERTS,), jnp.int32).at[flat_e].add(1)
    starts = jnp.cumsum(counts) - counts
    pcounts = (counts + MOE_CHUNK - 1) // MOE_CHUNK * MOE_CHUNK
    pends = jnp.cumsum(pcounts)
    pstarts = pends - pcounts
    dest = pstarts[se] + (jnp.arange(nk, dtype=jnp.int32) - starts[se])
    n_chunks = -(-nk // MOE_CHUNK) + N_EXPERTS
    n_slots = n_chunks * MOE_CHUNK
    slot_tok = jnp.zeros((n_slots,), jnp.int32).at[dest].set(flat_tok[order])
    slot_w = jnp.zeros((n_slots,), jnp.float32).at[dest].set(flat_w[order])
    chunk_start = jnp.arange(n_chunks, dtype=jnp.int32) * MOE_CHUNK
    chunk_e = jnp.clip(jnp.searchsorted(pends, chunk_start, side='right'), 0, N_EXPERTS - 1)
    xs = t[slot_tok].reshape(n_chunks, MOE_CHUNK, d)

    def expert_block(args):
        xc, e = args
        gu = xc @ w_gate_up[e] + b_gate_up[e]
        g, u = gu[:, :D_EXPERT], gu[:, D_EXPERT:]
        g = jnp.minimum(g, SWIGLU_LIMIT)
        u = jnp.clip(u, -SWIGLU_LIMIT, SWIGLU_LIMIT)
        act = (u + 1.0) * (g * jax.nn.sigmoid(SWIGLU_ALPHA * g))
        return act @ w_down[e] + b_down[e]

    ys = lax.map(expert_block, (xs, chunk_e)).reshape(n_slots, d)
    out = jnp.zeros((n, d), jnp.float32).at[slot_tok].add(ys.astype(jnp.float32) * slot_w[:, None])
    return out.astype(h.dtype).reshape(bsz, s, d)


def setup_inputs(seed: int = 0) -> dict:
    key = jax.random.key(seed)
    ks = jax.random.split(key, 24)
    f32 = jnp.float32
    nrm = lambda k, shape, scale: jax.random.normal(k, shape, f32) * scale
    a0 = jax.random.uniform(ks[9], (DEPTH, LRU_WIDTH), f32, 0.9, 0.999)
    s0 = a0 ** (1.0 / LRU_C)
    lru_lambda = jnp.log(s0) - jnp.log1p(-s0)
    return {
        "x": nrm(ks[0], (BATCH, SEQ, D_MODEL), 1.0),
        "positions": jnp.broadcast_to(jnp.arange(SEQ, dtype=jnp.int32), (BATCH, SEQ)),
        "norm_mix_g": 1.0 + nrm(ks[1], (DEPTH, D_MODEL), 0.02),
        "w_in": nrm(ks[2], (DEPTH, D_MODEL, IN_WIDTH), D_MODEL ** -0.5),
        "conv_w": nrm(ks[3], (DEPTH, CONV_WIDTH, LRU_WIDTH), CONV_WIDTH ** -0.5),
        "conv_b": nrm(ks[4], (DEPTH, LRU_WIDTH), 0.01),
        "w_rg": nrm(ks[5], (DEPTH, LRU_HEADS, LRU_BLOCK, LRU_BLOCK), LRU_BLOCK ** -0.5),
        "b_rg": nrm(ks[6], (DEPTH, LRU_WIDTH), 0.01),
        "w_ig": nrm(ks[7], (DEPTH, LRU_HEADS, LRU_BLOCK, LRU_BLOCK), LRU_BLOCK ** -0.5),
        "b_ig": nrm(ks[8], (DEPTH, LRU_WIDTH), 0.01),
        "lru_lambda": lru_lambda,
        "attn_sinks": nrm(ks[10], (DEPTH, N_Q_HEADS), 0.5),
        "w_attn_o": nrm(ks[11], (DEPTH, ATTN_WIDTH, D_MODEL), ATTN_WIDTH ** -0.5),
        "w_lru_o": nrm(ks[12], (DEPTH, LRU_WIDTH, D_MODEL), LRU_WIDTH ** -0.5),
        "w_out": nrm(ks[13], (DEPTH, D_MODEL, D_MODEL), D_MODEL ** -0.5),
        "norm_ffn_g": 1.0 + nrm(ks[14], (DEPTH, D_MODEL), 0.02),
        "w_router": nrm(ks[15], (DEPTH, D_MODEL, N_EXPERTS), D_MODEL ** -0.5),
        "b_router": nrm(ks[16], (DEPTH, N_EXPERTS), 0.01),
        "w_gate_up": nrm(ks[17], (DEPTH, N_EXPERTS, D_MODEL, 2 * D_EXPERT), D_MODEL ** -0.5),
        "b_gate_up": nrm(ks[18], (DEPTH, N_EXPERTS, 2 * D_EXPERT), 0.01),
        "w_down": nrm(ks[19], (DEPTH, N_EXPERTS, D_EXPERT, D_MODEL), D_EXPERT ** -0.5),
        "b_down": nrm(ks[20], (DEPTH, N_EXPERTS, D_MODEL), 0.01),
        "norm_final_g": 1.0 + nrm(ks[21], (D_MODEL,), 0.02),
    }


def reference(x, positions, norm_mix_g, w_in, conv_w, conv_b, w_rg, b_rg, w_ig, b_ig,
              lru_lambda, attn_sinks, w_attn_o, w_lru_o, w_out, norm_ffn_g, w_router,
              b_router, w_gate_up, b_gate_up, w_down, b_down, norm_final_g):
    for l in range(DEPTH):
        h = rmsnorm(x, norm_mix_g[l])
        x = x + mixer(h, positions, w_in[l], conv_w[l], conv_b[l], w_rg[l], b_rg[l],
                      w_ig[l], b_ig[l], lru_lambda[l], attn_sinks[l], w_attn_o[l],
                      w_lru_o[l], w_out[l])
        h = rmsnorm(x, norm_ffn_g[l])
        x = x + moe(h, w_router[l], b_router[l], w_gate_up[l], b_gate_up[l],
                    w_down[l], b_down[l])
    return rmsnorm(x, norm_final_g)
```

```python
import functools

import numpy as np
import jax
import jax.numpy as jnp
from jax import lax
from jax.experimental import pallas as pl
from jax.experimental.pallas import tpu as pltpu

F32 = jnp.float32
BF16 = jnp.bfloat16

D_MODEL = 1024
HEAD_DIM = 64
N_Q_HEADS = 8
N_KV_HEADS = 2
GROUP = N_Q_HEADS // N_KV_HEADS
ATTN_WIDTH = N_Q_HEADS * HEAD_DIM
KV_WIDTH = N_KV_HEADS * HEAD_DIM
WINDOW = 128
BLOCK = 128
ROPE_THETA = 10000.0
LRU_WIDTH = D_MODEL // 2
LRU_HEADS = 8
LRU_BLOCK = LRU_WIDTH // LRU_HEADS
LRU_C = 8.0
CONV_WIDTH = 4
N_EXPERTS = 32
TOP_K = 4
D_EXPERT = D_MODEL
SWIGLU_LIMIT = 7.0
SWIGLU_ALPHA = 1.702
NORM_EPS = 1e-6
NEG_INF = -1e30
IN_SPLITS = (ATTN_WIDTH, KV_WIDTH, KV_WIDTH, LRU_WIDTH, LRU_WIDTH, D_MODEL, D_MODEL)
IN_WIDTH = sum(IN_SPLITS)

LANES = 128
HALF = D_MODEL // 2
VMEM_LIMIT = 56 * 1024 * 1024

TOKEN_TILE = 512
LRU_TILE = 256
EXPERT_TILE = 512
ROW_DMA_TILE = 256


def _rms(x, g):
    return x * lax.rsqrt(jnp.mean(x * x, axis=-1, keepdims=True) + NORM_EPS) * g


def _pack_row(v):
    return pltpu.pack_elementwise([v[:, :HALF], v[:, HALF:]], packed_dtype=BF16)


def _unpack_row(w, index):
    return pltpu.unpack_elementwise(w, index=index, packed_dtype=BF16, unpacked_dtype=F32)


def _in_proj_kernel(x_ref, pos_ref, g_ref, invf_ref, w_ref,
                    q_ref, k_ref, v_ref, xr_ref, gy_ref, sga_ref, sgl_ref):
    h = _rms(x_ref[...], g_ref[...]).astype(BF16)

    def proj(lo, width):
        return jnp.dot(h, w_ref[:, lo:lo + width], preferred_element_type=F32)

    ang = pos_ref[...].astype(F32) * invf_ref[...]
    cos = jnp.cos(ang)
    sin = jnp.sin(ang)
    lane = lax.broadcasted_iota(jnp.int32, (1, LANES), 1)
    first_half = (lane % HEAD_DIM) < (HEAD_DIM // 2)
    sin_signed = jnp.where(first_half, -sin, sin)

    def rope(t):
        width = t.shape[-1]
        reps = width // LANES
        c = jnp.concatenate([cos] * reps, axis=-1) if reps > 1 else cos
        s = jnp.concatenate([sin_signed] * reps, axis=-1) if reps > 1 else sin_signed
        fh = jnp.concatenate([first_half] * reps, axis=-1) if reps > 1 else first_half
        nxt = pltpu.roll(t, width - HEAD_DIM // 2, axis=1)
        prv = pltpu.roll(t, HEAD_DIM // 2, axis=1)
        return t * c + jnp.where(fh, nxt, prv) * s

    o = 0
    q_ref[...] = (rope(proj(o, ATTN_WIDTH)) * (HEAD_DIM ** -0.5)).astype(BF16)
    o += ATTN_WIDTH
    k_ref[...] = rope(proj(o, KV_WIDTH)).astype(BF16)
    o += KV_WIDTH
    v_ref[...] = proj(o, KV_WIDTH).astype(BF16)
    o += KV_WIDTH
    xr_ref[...] = proj(o, LRU_WIDTH).astype(BF16)
    o += LRU_WIDTH
    gy_ref[...] = jax.nn.gelu(proj(o, LRU_WIDTH)).astype(BF16)
    o += LRU_WIDTH
    sga_ref[...] = jax.nn.sigmoid(proj(o, D_MODEL)).astype(BF16)
    o += D_MODEL
    sgl_ref[...] = jax.nn.sigmoid(proj(o, D_MODEL)).astype(BF16)


def _in_proj(x2, pos2, g, invf, w_in_bf, tm):
    n = x2.shape[0]
    row = lambda w: pl.BlockSpec((tm, w), lambda i: (i, 0))
    full = lambda a: pl.BlockSpec(a.shape, lambda i: (0,) * a.ndim)
    widths = (ATTN_WIDTH, KV_WIDTH, KV_WIDTH, LRU_WIDTH, LRU_WIDTH, D_MODEL, D_MODEL)
    return pl.pallas_call(
        _in_proj_kernel,
        grid=(n // tm,),
        in_specs=[row(D_MODEL), row(1), full(g), full(invf), full(w_in_bf)],
        out_specs=[row(w) for w in widths],
        out_shape=[jax.ShapeDtypeStruct((n, w), BF16) for w in widths],
        compiler_params=pltpu.CompilerParams(
            dimension_semantics=("parallel",), vmem_limit_bytes=VMEM_LIMIT),
        name="in_proj",
    )(x2, pos2, g, invf, w_in_bf)


def _attn_kernel(sink_ref, q_ref, kc_ref, kp_ref, vc_ref, vp_ref, o_ref):
    j = pl.program_id(1)
    q = q_ref[...]
    kk = jnp.concatenate([kp_ref[...], kc_ref[...]], axis=0)
    vv = jnp.concatenate([vp_ref[...], vc_ref[...]], axis=0)
    row = lax.broadcasted_iota(jnp.int32, (BLOCK, 2 * BLOCK), 0)
    col = lax.broadcasted_iota(jnp.int32, (BLOCK, 2 * BLOCK), 1)
    mask = (col > row) & (col <= row + BLOCK) & ((col >= BLOCK) | (j > 0))
    outs = []
    for h in range(N_Q_HEADS):
        g = h // GROUP
        qh = q[:, h * HEAD_DIM:(h + 1) * HEAD_DIM]
        kg = kk[:, g * HEAD_DIM:(g + 1) * HEAD_DIM]
        vg = vv[:, g * HEAD_DIM:(g + 1) * HEAD_DIM]
        s = lax.dot_general(qh, kg, (((1,), (1,)), ((), ())), preferred_element_type=F32)
        s = jnp.where(mask, s, NEG_INF)
        sink = sink_ref[h]
        m = jnp.maximum(jnp.max(s, axis=-1, keepdims=True), sink)
        p = jnp.exp(s - m)
        denom = jnp.sum(p, axis=-1, keepdims=True) + jnp.exp(sink - m)
        oh = jnp.dot(p.astype(BF16), vg, preferred_element_type=F32)
        outs.append(oh / denom)
    o_ref[...] = jnp.concatenate(outs, axis=-1).astype(BF16)


def _attention(sinks, q, k, v, bsz, seq):
    n = q.shape[0]
    nb = seq // BLOCK
    cur = lambda w: pl.BlockSpec((BLOCK, w), lambda b, j: (b * nb + j, 0))
    prev = lambda w: pl.BlockSpec((BLOCK, w), lambda b, j: (b * nb + jnp.maximum(j - 1, 0), 0))
    return pl.pallas_call(
        _attn_kernel,
        grid=(bsz, nb),
        in_specs=[pl.BlockSpec(memory_space=pltpu.SMEM),
                  cur(ATTN_WIDTH), cur(KV_WIDTH), prev(KV_WIDTH), cur(KV_WIDTH), prev(KV_WIDTH)],
        out_specs=cur(ATTN_WIDTH),
        out_shape=jax.ShapeDtypeStruct((n, ATTN_WIDTH), BF16),
        compiler_params=pltpu.CompilerParams(
            dimension_semantics=("parallel", "parallel"), vmem_limit_bytes=VMEM_LIMIT),
        name="attn",
    )(sinks, q, k, k, v, v)


def _lru_kernel(xr_ref, gy_ref, cw_ref, cb_ref, wg_ref, bg_ref, lam_ref, o_ref,
                tail_ref, h_ref):
    ts = xr_ref.shape[0]

    @pl.when(pl.program_id(1) == 0)
    def _():
        tail_ref[...] = jnp.zeros_like(tail_ref)
        h_ref[...] = jnp.zeros_like(h_ref)

    x = xr_ref[...].astype(F32)
    prev = tail_ref[...]
    row8 = lax.broadcasted_iota(jnp.int32, (8, LRU_WIDTH), 0)
    cw = cw_ref[...]
    xc = x * cw[CONV_WIDTH - 1:CONV_WIDTH] + cb_ref[...]
    for d in range(1, CONV_WIDTH):
        xs = pltpu.roll(x, d, axis=0)
        ps = pltpu.roll(prev, d, axis=0)
        head = jnp.where(row8 < d, ps, xs[0:8])
        xs = jnp.concatenate([head, xs[8:]], axis=0)
        xc = xc + xs * cw[CONV_WIDTH - 1 - d:CONV_WIDTH - d]
    tail_ref[...] = x[ts - 8:ts]

    gates = jnp.dot(xc.astype(BF16), wg_ref[...], preferred_element_type=F32) + bg_ref[...]
    r = jax.nn.sigmoid(gates[:, :LRU_WIDTH])
    i = jax.nn.sigmoid(gates[:, LRU_WIDTH:])
    log_a = (-LRU_C) * r * jax.nn.softplus(-lam_ref[...])
    a = jnp.exp(log_a)
    u = jnp.sqrt(1.0 - a * a) * (i * xc)

    rows = lax.broadcasted_iota(jnp.int32, (ts, LRU_WIDTH), 0)
    d = 1
    while d < ts:
        keep = rows >= d
        a_s = jnp.where(keep, pltpu.roll(a, d, axis=0), 1.0)
        u_s = jnp.where(keep, pltpu.roll(u, d, axis=0), 0.0)
        u = a * u_s + u
        a = a * a_s
        d *= 2
    h = a * h_ref[0:1] + u
    h_ref[...] = jnp.broadcast_to(h[ts - 1:ts], h_ref.shape)
    o_ref[...] = (h * gy_ref[...].astype(F32)).astype(BF16)


def _lru(xr, gy, cw, cb, wg, bg, lam, bsz, seq, ts):
    n = xr.shape[0]
    nt = seq // ts
    tile = pl.BlockSpec((ts, LRU_WIDTH), lambda b, j: (b * nt + j, 0))
    full = lambda a: pl.BlockSpec(a.shape, lambda b, j: (0,) * a.ndim)
    return pl.pallas_call(
        _lru_kernel,
        grid=(bsz, nt),
        in_specs=[tile, tile, full(cw), full(cb), full(wg), full(bg), full(lam)],
        out_specs=tile,
        out_shape=jax.ShapeDtypeStruct((n, LRU_WIDTH), BF16),
        scratch_shapes=[pltpu.VMEM((8, LRU_WIDTH), F32), pltpu.VMEM((8, LRU_WIDTH), F32)],
        compiler_params=pltpu.CompilerParams(
            dimension_semantics=("parallel", "arbitrary"), vmem_limit_bytes=VMEM_LIMIT),
        name="lru",
    )(xr, gy, cw, cb, wg, bg, lam)


def _merge_kernel(o_ref, hl_ref, sga_ref, sgl_ref, x_ref, wao_ref, wlo_ref, wout_ref,
                  g_ref, wr_ref, br_ref,
                  x1_ref, h2p_ref, idx_ref, gate_ref, rank_ref, cnt_ref, carry_ref):
    tm = x_ref.shape[0]

    @pl.when(pl.program_id(0) == 0)
    def _():
        carry_ref[...] = jnp.zeros_like(carry_ref)

    ya = jnp.dot(o_ref[...], wao_ref[...], preferred_element_type=F32)
    yl = jnp.dot(hl_ref[...], wlo_ref[...], preferred_element_type=F32)
    merged = sga_ref[...].astype(F32) * ya + sgl_ref[...].astype(F32) * yl
    x1 = x_ref[...] + jnp.dot(merged.astype(BF16), wout_ref[...], preferred_element_type=F32)
    x1_ref[...] = x1
    h2 = _rms(x1, g_ref[...])
    h2p_ref[...] = _pack_row(h2)

    logits = jnp.dot(h2, wr_ref[...], preferred_element_type=F32,
                     precision=lax.Precision.HIGHEST) + br_ref[...]
    lane = lax.broadcasted_iota(jnp.int32, (tm, LANES), 1)
    work = logits
    vals, sels = [], []
    for _ in range(TOP_K):
        m = jnp.max(work, axis=-1, keepdims=True)
        sel = jnp.min(jnp.where(work == m, lane, LANES), axis=-1, keepdims=True)
        vals.append(m)
        sels.append(sel)
        work = jnp.where(lane == sel, -jnp.inf, work)
    exps = [jnp.exp(v - vals[0]) for v in vals]
    tot = exps[0] + exps[1] + exps[2] + exps[3]

    onehot = jnp.zeros((tm, LANES), F32)
    for sel in sels:
        onehot = onehot + (lane == sel).astype(F32)
    r_i = lax.broadcasted_iota(jnp.int32, (tm, tm), 0)
    c_i = lax.broadcasted_iota(jnp.int32, (tm, tm), 1)
    tri = (c_i < r_i).astype(BF16)
    carry = carry_ref[...]
    before = jnp.dot(tri, onehot.astype(BF16), preferred_element_type=F32) + carry
    carry = carry + jnp.sum(onehot, axis=0, keepdims=True)
    carry_ref[...] = carry
    cnt_ref[...] = carry

    idx_o = jnp.zeros((tm, LANES), jnp.int32)
    gate_o = jnp.zeros((tm, LANES), F32)
    rank_o = jnp.zeros((tm, LANES), jnp.int32)
    for k in range(TOP_K):
        rk = jnp.sum(jnp.where(lane == sels[k], before, 0.0), axis=-1, keepdims=True)
        idx_o = jnp.where(lane == k, sels[k], idx_o)
        gate_o = jnp.where(lane == k, exps[k] / tot, gate_o)
        rank_o = jnp.where(lane == k, rk.astype(jnp.int32), rank_o)
    idx_ref[...] = idx_o
    gate_ref[...] = gate_o
    rank_ref[...] = rank_o


def _merge(o, hl, sga, sgl, x2, wao, wlo, wout, g, wr, br, tm):
    n = x2.shape[0]
    row = lambda w: pl.BlockSpec((tm, w), lambda i: (i, 0))
    full = lambda a: pl.BlockSpec(a.shape, lambda i: (0,) * a.ndim)
    return pl.pallas_call(
        _merge_kernel,
        grid=(n // tm,),
        in_specs=[row(ATTN_WIDTH), row(LRU_WIDTH), row(D_MODEL), row(D_MODEL), row(D_MODEL),
                  full(wao), full(wlo), full(wout), full(g), full(wr), full(br)],
        out_specs=[row(D_MODEL), row(HALF), row(LANES), row(LANES), row(LANES),
                   pl.BlockSpec((1, LANES), lambda i: (0, 0))],
        out_shape=[jax.ShapeDtypeStruct((n, D_MODEL), F32),
                   jax.ShapeDtypeStruct((n, HALF), jnp.uint32),
                   jax.ShapeDtypeStruct((n, LANES), jnp.int32),
                   jax.ShapeDtypeStruct((n, LANES), F32),
                   jax.ShapeDtypeStruct((n, LANES), jnp.int32),
                   jax.ShapeDtypeStruct((1, LANES), F32)],
        scratch_shapes=[pltpu.VMEM((1, LANES), F32)],
        compiler_params=pltpu.CompilerParams(
            dimension_semantics=("arbitrary",), vmem_limit_bytes=VMEM_LIMIT),
        name="merge_router",
    )(o, hl, sga, sgl, x2, wao, wlo, wout, g, wr, br)


def _dispatch_kernel(dest_ref, h_ref, xs_in_ref, xs_ref, sem):
    del xs_in_ref
    td = h_ref.shape[0]
    base = pl.program_id(0) * (td * TOP_K)

    def row_copy(r, k):
        d = dest_ref[base + r * TOP_K + k]
        return pltpu.make_async_copy(h_ref.at[pl.ds(r, 1)], xs_ref.at[pl.ds(d, 1)], sem)

    def start(r, c):
        for k in range(TOP_K):
            row_copy(r, k).start()
        return c

    def wait(r, c):
        for k in range(TOP_K):
            row_copy(r, k).wait()
        return c

    lax.fori_loop(0, td, start, 0)
    lax.fori_loop(0, td, wait, 0)


def _dispatch(dest, h2p, n_slots, td):
    n = h2p.shape[0]
    xs0 = jnp.zeros((n_slots, HALF), jnp.uint32)
    return pl.pallas_call(
        _dispatch_kernel,
        grid_spec=pltpu.PrefetchScalarGridSpec(
            num_scalar_prefetch=1,
            grid=(n // td,),
            in_specs=[pl.BlockSpec((td, HALF), lambda i, d: (i, 0)),
                      pl.BlockSpec(memory_space=pl.ANY)],
            out_specs=pl.BlockSpec(memory_space=pl.ANY),
            scratch_shapes=[pltpu.SemaphoreType.DMA(())],
        ),
        out_shape=jax.ShapeDtypeStruct((n_slots, HALF), jnp.uint32),
        input_output_aliases={2: 0},
        compiler_params=pltpu.CompilerParams(
            dimension_semantics=("arbitrary",), vmem_limit_bytes=VMEM_LIMIT),
        name="dispatch",
    )(dest, h2p, xs0)


def _expert_kernel(te_ref, na_ref, xs_ref, wgu_ref, bgu_ref, wd_ref, bd_ref, ys_ref):
    del te_ref

    @pl.when(pl.program_id(0) < na_ref[0])
    def _():
        w = xs_ref[...]
        lo = _unpack_row(w, 0).astype(BF16)
        hi = _unpack_row(w, 1).astype(BF16)
        gu = (jnp.dot(lo, wgu_ref[0, :HALF, :], preferred_element_type=F32)
              + jnp.dot(hi, wgu_ref[0, HALF:, :], preferred_element_type=F32)
              + bgu_ref[0])
        g = jnp.minimum(gu[:, :D_EXPERT], SWIGLU_LIMIT)
        u = jnp.clip(gu[:, D_EXPERT:], -SWIGLU_LIMIT, SWIGLU_LIMIT)
        act = (u + 1.0) * (g * jax.nn.sigmoid(SWIGLU_ALPHA * g))
        y = jnp.dot(act.astype(BF16), wd_ref[0], preferred_element_type=F32) + bd_ref[0]
        ys_ref[...] = _pack_row(y)


def _experts(tile_e, n_active, xs, wgu, bgu, wd, bd, tile_rows):
    n_slots = xs.shape[0]
    n_tiles = n_slots // tile_rows
    rows = pl.BlockSpec((tile_rows, HALF),
                        lambda i, te, na: (jnp.minimum(i, na[0] - 1), 0))
    per_e = lambda a: pl.BlockSpec((1,) + a.shape[1:], lambda i, te, na: (te[i], 0, 0))
    return pl.pallas_call(
        _expert_kernel,
        grid_spec=pltpu.PrefetchScalarGridSpec(
            num_scalar_prefetch=2,
            grid=(n_tiles,),
            in_specs=[rows, per_e(wgu), per_e(bgu), per_e(wd), per_e(bd)],
            out_specs=rows,
        ),
        out_shape=jax.ShapeDtypeStruct((n_slots, HALF), jnp.uint32),
        input_output_aliases={2: 0},
        compiler_params=pltpu.CompilerParams(
            dimension_semantics=("arbitrary",), vmem_limit_bytes=VMEM_LIMIT),
        name="experts",
    )(tile_e, n_active, xs, wgu, bgu, wd, bd)


def _combine_kernel(dest_ref, gate_ref, x1_ref, g_ref, ys_ref, o_ref, buf, sem):
    tc = x1_ref.shape[0]
    base = pl.program_id(0) * (tc * TOP_K)

    def row_copy(r, k):
        d = dest_ref[base + r * TOP_K + k]
        return pltpu.make_async_copy(ys_ref.at[pl.ds(d, 1)], buf.at[k, pl.ds(r, 1)], sem)

    def start(r, c):
        for k in range(TOP_K):
            row_copy(r, k).start()
        return c

    def wait(r, c):
        for k in range(TOP_K):
            row_copy(r, k).wait()
        return c

    lax.fori_loop(0, tc, start, 0)
    lax.fori_loop(0, tc, wait, 0)

    gate = gate_ref[...]
    lo = jnp.zeros((tc, HALF), F32)
    hi = jnp.zeros((tc, HALF), F32)
    for k in range(TOP_K):
        w = buf[k]
        gk = gate[:, k:k + 1]
        lo = lo + gk * _unpack_row(w, 0)
        hi = hi + gk * _unpack_row(w, 1)
    x2 = x1_ref[...] + jnp.concatenate([lo, hi], axis=-1)
    o_ref[...] = _rms(x2, g_ref[...])


def _combine(dest, gate, x1, g, ys, tc):
    n = x1.shape[0]
    return pl.pallas_call(
        _combine_kernel,
        grid_spec=pltpu.PrefetchScalarGridSpec(
            num_scalar_prefetch=1,
            grid=(n // tc,),
            in_specs=[pl.BlockSpec((tc, LANES), lambda i, d: (i, 0)),
                      pl.BlockSpec((tc, D_MODEL), lambda i, d: (i, 0)),
                      pl.BlockSpec((1, D_MODEL), lambda i, d: (0, 0)),
                      pl.BlockSpec(memory_space=pl.ANY)],
            out_specs=pl.BlockSpec((tc, D_MODEL), lambda i, d: (i, 0)),
            scratch_shapes=[pltpu.VMEM((TOP_K, tc, HALF), jnp.uint32),
                            pltpu.SemaphoreType.DMA(())],
        ),
        out_shape=jax.ShapeDtypeStruct((n, D_MODEL), F32),
        compiler_params=pltpu.CompilerParams(
            dimension_semantics=("arbitrary",), vmem_limit_bytes=VMEM_LIMIT),
        name="combine",
    )(dest, gate, x1, g, ys)


def _block_diag(w):
    eye = jnp.eye(LRU_HEADS, dtype=w.dtype)
    return jnp.einsum("hij,hg->higj", w, eye).reshape(LRU_WIDTH, LRU_WIDTH)


def _pick(n, pref):
    t = pref
    while n % t:
        t //= 2
    return t


def _layer(x2, pos2, bsz, seq, norm_mix_g, w_in, conv_w, conv_b, w_rg, b_rg, w_ig, b_ig,
           lru_lambda, attn_sinks, w_attn_o, w_lru_o, w_out, norm_ffn_g, w_router, b_router,
           w_gate_up, b_gate_up, w_down, b_down, norm_out_g):
    n = x2.shape[0]
    tm = _pick(n, TOKEN_TILE)
    row = lambda a: a.reshape(1, -1).astype(F32)

    half = np.arange(0, HEAD_DIM, 2, dtype=np.float32) / HEAD_DIM
    inv_freq = (1.0 / (ROPE_THETA ** jnp.asarray(half, F32)))
    invf = jnp.tile(inv_freq, LANES // (HEAD_DIM // 2)).reshape(1, LANES)

    q, k, v, xr, gy, sga, sgl = _in_proj(x2, pos2, row(norm_mix_g), invf, w_in.astype(BF16), tm)
    o = _attention(attn_sinks.astype(F32), q, k, v, bsz, seq)

    wg = jnp.concatenate([_block_diag(w_rg), _block_diag(w_ig)], axis=1).astype(BF16)
    bg = jnp.concatenate([b_rg, b_ig]).reshape(1, -1).astype(F32)
    hl = _lru(xr, gy, conv_w.astype(F32), row(conv_b), wg, bg, row(lru_lambda),
              bsz, seq, _pick(seq, LRU_TILE))

    wr = jnp.zeros((D_MODEL, LANES), F32).at[:, :N_EXPERTS].set(w_router.astype(F32))
    br = jnp.full((1, LANES), NEG_INF, F32).at[0, :N_EXPERTS].set(b_router.astype(F32))
    x1, h2p, idx, gate, rank, cnt = _merge(
        o, hl, sga, sgl, x2, w_attn_o.astype(BF16), w_lru_o.astype(BF16), w_out.astype(BF16),
        row(norm_ffn_g), wr, br, tm)

    nk = n * TOP_K
    te_rows = EXPERT_TILE
    n_tiles = -(-nk // te_rows) + N_EXPERTS
    counts = cnt[0, :N_EXPERTS].astype(jnp.int32)
    pcounts = (counts + te_rows - 1) // te_rows * te_rows
    pends = jnp.cumsum(pcounts)
    pstarts = pends - pcounts
    dest = (pstarts[idx[:, :TOP_K]] + rank[:, :TOP_K]).reshape(-1).astype(jnp.int32)
    n_active = (pends[-1] // te_rows).astype(jnp.int32)
    tile_ids = jnp.minimum(jnp.arange(n_tiles, dtype=jnp.int32), n_active - 1)
    tile_e = jnp.clip(jnp.searchsorted(pends, tile_ids * te_rows, side="right"),
                      0, N_EXPERTS - 1).astype(jnp.int32)

    td = _pick(n, ROW_DMA_TILE)
    xs = _dispatch(dest, h2p, n_tiles * te_rows, td)
    ys = _experts(tile_e, n_active.reshape(1), xs,
                  w_gate_up.astype(BF16), b_gate_up.reshape(N_EXPERTS, 1, -1).astype(F32),
                  w_down.astype(BF16), b_down.reshape(N_EXPERTS, 1, -1).astype(F32), te_rows)
    return _combine(dest, gate, x1, row(norm_out_g), ys, td)


def kernel(x, positions, norm_mix_g, w_in, conv_w, conv_b, w_rg, b_rg, w_ig, b_ig, lru_lambda,
           attn_sinks, w_attn_o, w_lru_o, w_out, norm_ffn_g, w_router, b_router, w_gate_up,
           b_gate_up, w_down, b_down, norm_final_g):
    bsz, seq, d = x.shape
    depth = w_in.shape[0]
    assert depth == 1 and d == D_MODEL and seq % BLOCK == 0
    x2 = x.reshape(bsz * seq, d)
    pos2 = positions.reshape(bsz * seq, 1).astype(jnp.int32)
    out = _layer(x2, pos2, bsz, seq, norm_mix_g[0], w_in[0], conv_w[0], conv_b[0], w_rg[0],
                 b_rg[0], w_ig[0], b_ig[0], lru_lambda[0], attn_sinks[0], w_attn_o[0],
                 w_lru_o[0], w_out[0], norm_ffn_g[0], w_router[0], b_router[0], w_gate_up[0],
                 b_gate_up[0], w_down[0], b_down[0], norm_final_g)
    return out.reshape(bsz, seq, d)
```

```python
import numpy as np
import jax
import jax.numpy as jnp
from jax import lax
from jax.experimental import pallas as pl
from jax.experimental.pallas import tpu as pltpu

F32 = jnp.float32
BF16 = jnp.bfloat16

D_MODEL = 1024
HEAD_DIM = 64
N_Q_HEADS = 8
N_KV_HEADS = 2
GROUP = N_Q_HEADS // N_KV_HEADS
ATTN_WIDTH = N_Q_HEADS * HEAD_DIM
KV_WIDTH = N_KV_HEADS * HEAD_DIM
WINDOW = 128
BLOCK = 128
ROPE_THETA = 10000.0
LRU_WIDTH = D_MODEL // 2
LRU_HEADS = 8
LRU_BLOCK = LRU_WIDTH // LRU_HEADS
LRU_C = 8.0
CONV_WIDTH = 4
N_EXPERTS = 32
TOP_K = 4
D_EXPERT = D_MODEL
SWIGLU_LIMIT = 7.0
SWIGLU_ALPHA = 1.702
NORM_EPS = 1e-6
NEG_INF = -1e30
IN_SPLITS = (ATTN_WIDTH, KV_WIDTH, KV_WIDTH, LRU_WIDTH, LRU_WIDTH, D_MODEL, D_MODEL)
IN_WIDTH = sum(IN_SPLITS)

LANES = 128
HALF = D_MODEL // 2
VMEM_LIMIT = 56 * 1024 * 1024

TOKEN_TILE = 512
LRU_TILE = 256
EXPERT_TILE = 512
CHUNK = 8
PERM_SPLIT = 4


def _rms(x, g):
    return x * lax.rsqrt(jnp.mean(x * x, axis=-1, keepdims=True) + NORM_EPS) * g


def _pack_row(v):
    return pltpu.pack_elementwise([v[:, :HALF], v[:, HALF:]], packed_dtype=BF16)


def _unpack_row(w, index):
    return pltpu.unpack_elementwise(w, index=index, packed_dtype=BF16, unpacked_dtype=F32)


def _in_proj_kernel(x_ref, pos_ref, g_ref, invf_ref, w_ref,
                    q_ref, k_ref, v_ref, xr_ref, gy_ref, sga_ref, sgl_ref):
    h = _rms(x_ref[...], g_ref[...]).astype(BF16)

    def proj(lo, width):
        return jnp.dot(h, w_ref[:, lo:lo + width], preferred_element_type=F32)

    ang = pos_ref[...].astype(F32) * invf_ref[...]
    cos = jnp.cos(ang)
    sin = jnp.sin(ang)
    lane = lax.broadcasted_iota(jnp.int32, (1, LANES), 1)
    first_half = (lane % HEAD_DIM) < (HEAD_DIM // 2)
    sin_signed = jnp.where(first_half, -sin, sin)

    def rope(t):
        width = t.shape[-1]
        reps = width // LANES
        c = jnp.concatenate([cos] * reps, axis=-1) if reps > 1 else cos
        s = jnp.concatenate([sin_signed] * reps, axis=-1) if reps > 1 else sin_signed
        fh = jnp.concatenate([first_half] * reps, axis=-1) if reps > 1 else first_half
        nxt = pltpu.roll(t, width - HEAD_DIM // 2, axis=1)
        prv = pltpu.roll(t, HEAD_DIM // 2, axis=1)
        return t * c + jnp.where(fh, nxt, prv) * s

    o = 0
    q_ref[...] = (rope(proj(o, ATTN_WIDTH)) * (HEAD_DIM ** -0.5)).astype(BF16)
    o += ATTN_WIDTH
    k_ref[...] = rope(proj(o, KV_WIDTH)).astype(BF16)
    o += KV_WIDTH
    v_ref[...] = proj(o, KV_WIDTH).astype(BF16)
    o += KV_WIDTH
    xr_ref[...] = proj(o, LRU_WIDTH).astype(BF16)
    o += LRU_WIDTH
    gy_ref[...] = jax.nn.gelu(proj(o, LRU_WIDTH)).astype(BF16)
    o += LRU_WIDTH
    sga_ref[...] = jax.nn.sigmoid(proj(o, D_MODEL)).astype(BF16)
    o += D_MODEL
    sgl_ref[...] = jax.nn.sigmoid(proj(o, D_MODEL)).astype(BF16)


def _in_proj(x2, pos2, g, invf, w_in_bf, tm):
    n = x2.shape[0]
    row = lambda w: pl.BlockSpec((tm, w), lambda i: (i, 0))
    full = lambda a: pl.BlockSpec(a.shape, lambda i: (0,) * a.ndim)
    widths = (ATTN_WIDTH, KV_WIDTH, KV_WIDTH, LRU_WIDTH, LRU_WIDTH, D_MODEL, D_MODEL)
    return pl.pallas_call(
        _in_proj_kernel,
        grid=(n // tm,),
        in_specs=[row(D_MODEL), row(1), full(g), full(invf), full(w_in_bf)],
        out_specs=[row(w) for w in widths],
        out_shape=[jax.ShapeDtypeStruct((n, w), BF16) for w in widths],
        compiler_params=pltpu.CompilerParams(
            dimension_semantics=("parallel",), vmem_limit_bytes=VMEM_LIMIT),
        name="in_proj",
    )(x2, pos2, g, invf, w_in_bf)


def _attn_kernel(sink_ref, q_ref, kc_ref, kp_ref, vc_ref, vp_ref, o_ref):
    j = pl.program_id(1)
    q = q_ref[...]
    kk = jnp.concatenate([kp_ref[...], kc_ref[...]], axis=0)
    vv = jnp.concatenate([vp_ref[...], vc_ref[...]], axis=0)
    row = lax.broadcasted_iota(jnp.int32, (BLOCK, 2 * BLOCK), 0)
    col = lax.broadcasted_iota(jnp.int32, (BLOCK, 2 * BLOCK), 1)
    mask = (col > row) & (col <= row + BLOCK) & ((col >= BLOCK) | (j > 0))
    outs = []
    for h in range(N_Q_HEADS):
        g = h // GROUP
        qh = q[:, h * HEAD_DIM:(h + 1) * HEAD_DIM]
        kg = kk[:, g * HEAD_DIM:(g + 1) * HEAD_DIM]
        vg = vv[:, g * HEAD_DIM:(g + 1) * HEAD_DIM]
        s = lax.dot_general(qh, kg, (((1,), (1,)), ((), ())), preferred_element_type=F32)
        s = jnp.where(mask, s, NEG_INF)
        sink = sink_ref[h]
        m = jnp.maximum(jnp.max(s, axis=-1, keepdims=True), sink)
        p = jnp.exp(s - m)
        denom = jnp.sum(p, axis=-1, keepdims=True) + jnp.exp(sink - m)
        oh = jnp.dot(p.astype(BF16), vg, preferred_element_type=F32)
        outs.append(oh / denom)
    o_ref[...] = jnp.concatenate(outs, axis=-1).astype(BF16)


def _attention(sinks, q, k, v, bsz, seq):
    n = q.shape[0]
    nb = seq // BLOCK
    cur = lambda w: pl.BlockSpec((BLOCK, w), lambda b, j: (b * nb + j, 0))
    prev = lambda w: pl.BlockSpec((BLOCK, w), lambda b, j: (b * nb + jnp.maximum(j - 1, 0), 0))
    return pl.pallas_call(
        _attn_kernel,
        grid=(bsz, nb),
        in_specs=[pl.BlockSpec(memory_space=pltpu.SMEM),
                  cur(ATTN_WIDTH), cur(KV_WIDTH), prev(KV_WIDTH), cur(KV_WIDTH), prev(KV_WIDTH)],
        out_specs=cur(ATTN_WIDTH),
        out_shape=jax.ShapeDtypeStruct((n, ATTN_WIDTH), BF16),
        compiler_params=pltpu.CompilerParams(
            dimension_semantics=("parallel", "parallel"), vmem_limit_bytes=VMEM_LIMIT),
        name="attn",
    )(sinks, q, k, k, v, v)


def _lru_kernel(xr_ref, gy_ref, cw_ref, cb_ref, wg_ref, bg_ref, lam_ref, o_ref,
                tail_ref, h_ref):
    ts = xr_ref.shape[0]

    @pl.when(pl.program_id(1) == 0)
    def _():
        tail_ref[...] = jnp.zeros_like(tail_ref)
        h_ref[...] = jnp.zeros_like(h_ref)

    x = xr_ref[...].astype(F32)
    prev = tail_ref[...]
    row8 = lax.broadcasted_iota(jnp.int32, (8, LRU_WIDTH), 0)
    cw = cw_ref[...]
    xc = x * cw[CONV_WIDTH - 1:CONV_WIDTH] + cb_ref[...]
    for d in range(1, CONV_WIDTH):
        xs = pltpu.roll(x, d, axis=0)
        ps = pltpu.roll(prev, d, axis=0)
        head = jnp.where(row8 < d, ps, xs[0:8])
        xs = jnp.concatenate([head, xs[8:]], axis=0)
        xc = xc + xs * cw[CONV_WIDTH - 1 - d:CONV_WIDTH - d]
    tail_ref[...] = x[ts - 8:ts]

    gates = jnp.dot(xc.astype(BF16), wg_ref[...], preferred_element_type=F32) + bg_ref[...]
    r = jax.nn.sigmoid(gates[:, :LRU_WIDTH])
    i = jax.nn.sigmoid(gates[:, LRU_WIDTH:])
    log_a = (-LRU_C) * r * jax.nn.softplus(-lam_ref[...])
    a = jnp.exp(log_a)
    u = jnp.sqrt(1.0 - a * a) * (i * xc)

    rows = lax.broadcasted_iota(jnp.int32, (ts, LRU_WIDTH), 0)
    d = 1
    while d < ts:
        keep = rows >= d
        a_s = jnp.where(keep, pltpu.roll(a, d, axis=0), 1.0)
        u_s = jnp.where(keep, pltpu.roll(u, d, axis=0), 0.0)
        u = a * u_s + u
        a = a * a_s
        d *= 2
    h = a * h_ref[0:1] + u
    h_ref[...] = jnp.broadcast_to(h[ts - 1:ts], h_ref.shape)
    o_ref[...] = (h * gy_ref[...].astype(F32)).astype(BF16)


def _lru(xr, gy, cw, cb, wg, bg, lam, bsz, seq, ts):
    n = xr.shape[0]
    nt = seq // ts
    tile = pl.BlockSpec((ts, LRU_WIDTH), lambda b, j: (b * nt + j, 0))
    full = lambda a: pl.BlockSpec(a.shape, lambda b, j: (0,) * a.ndim)
    return pl.pallas_call(
        _lru_kernel,
        grid=(bsz, nt),
        in_specs=[tile, tile, full(cw), full(cb), full(wg), full(bg), full(lam)],
        out_specs=tile,
        out_shape=jax.ShapeDtypeStruct((n, LRU_WIDTH), BF16),
        scratch_shapes=[pltpu.VMEM((8, LRU_WIDTH), F32), pltpu.VMEM((8, LRU_WIDTH), F32)],
        compiler_params=pltpu.CompilerParams(
            dimension_semantics=("parallel", "arbitrary"), vmem_limit_bytes=VMEM_LIMIT),
        name="lru",
    )(xr, gy, cw, cb, wg, bg, lam)


def _merge_kernel(o_ref, hl_ref, sga_ref, sgl_ref, x_ref, wao_ref, wlo_ref, wout_ref,
                  g_ref, wrh_ref, wrl_ref, br_ref,
                  x1_ref, hc_ref, pos_ref, gate_ref, cnt_ref):
    tm = x_ref.shape[0]
    cap = hc_ref.shape[0]

    ya = jnp.dot(o_ref[...], wao_ref[...], preferred_element_type=F32)
    yl = jnp.dot(hl_ref[...], wlo_ref[...], preferred_element_type=F32)
    merged = sga_ref[...].astype(F32) * ya + sgl_ref[...].astype(F32) * yl
    x1 = x_ref[...] + jnp.dot(merged.astype(BF16), wout_ref[...], preferred_element_type=F32)
    x1_ref[...] = x1
    h2 = _rms(x1, g_ref[...])
    h2b = h2.astype(BF16)

    h2l = (h2 - h2b.astype(F32)).astype(BF16)
    logits = (jnp.dot(h2b, wrh_ref[...], preferred_element_type=F32)
              + jnp.dot(h2l, wrh_ref[...], preferred_element_type=F32)
              + jnp.dot(h2b, wrl_ref[...], preferred_element_type=F32)
              + br_ref[...])
    lane = lax.broadcasted_iota(jnp.int32, (tm, LANES), 1)
    work = logits
    vals, sels = [], []
    for _ in range(TOP_K):
        m = jnp.max(work, axis=-1, keepdims=True)
        sel = jnp.min(jnp.where(work == m, lane, LANES), axis=-1, keepdims=True)
        vals.append(m)
        sels.append(sel)
        work = jnp.where(lane == sel, -jnp.inf, work)
    exps = [jnp.exp(v - vals[0]) for v in vals]
    tot = exps[0] + exps[1] + exps[2] + exps[3]

    onehot = jnp.zeros((tm, LANES), F32)
    for sel in sels:
        onehot = onehot + (lane == sel).astype(F32)
    r_i = lax.broadcasted_iota(jnp.int32, (tm, tm), 0)
    c_i = lax.broadcasted_iota(jnp.int32, (tm, tm), 1)
    earlier = jnp.dot((c_i < r_i).astype(BF16), onehot.astype(BF16),
                      preferred_element_type=F32)
    cnt = jnp.sum(onehot, axis=0, keepdims=True)
    chunks = jnp.floor((cnt + (CHUNK - 1)) * (1.0 / CHUNK))
    e_r = lax.broadcasted_iota(jnp.int32, (LANES, LANES), 0)
    e_c = lax.broadcasted_iota(jnp.int32, (LANES, LANES), 1)
    start = jnp.dot(jnp.broadcast_to(chunks, (8, LANES)).astype(BF16), (e_r < e_c).astype(BF16),
                    preferred_element_type=F32)[0:1] * float(CHUNK)
    slot_of = earlier + start

    pos_o = jnp.full((tm, LANES), -1.0, F32)
    gate_o = jnp.zeros((tm, LANES), F32)
    for k in range(TOP_K):
        pk = jnp.sum(jnp.where(lane == sels[k], slot_of, 0.0), axis=-1, keepdims=True)
        pos_o = jnp.where(lane == k, pk, pos_o)
        gate_o = jnp.where(lane == k, exps[k] / tot, gate_o)
    pos_ref[...] = pos_o.astype(jnp.int32)
    gate_ref[...] = gate_o
    cnt_ref[...] = jnp.broadcast_to(cnt, cnt_ref.shape)

    pos_t = jnp.transpose(pos_o).astype(jnp.int32)
    rows_per = cap // PERM_SPLIT
    for c in range(PERM_SPLIT):
        slot_id = lax.broadcasted_iota(jnp.int32, (rows_per, tm), 0) + c * rows_per
        perm = jnp.zeros((rows_per, tm), F32)
        for k in range(TOP_K):
            perm = jnp.where(slot_id == pos_t[k:k + 1], 1.0, perm)
        rows = jnp.dot(perm.astype(BF16), h2b, preferred_element_type=F32)
        hc_ref[c * rows_per:(c + 1) * rows_per, :] = _pack_row(rows)


def _merge(o, hl, sga, sgl, x2, wao, wlo, wout, g, wrh, wrl, br, tm, cap):
    n = x2.shape[0]
    row = lambda w: pl.BlockSpec((tm, w), lambda i: (i, 0))
    full = lambda a: pl.BlockSpec(a.shape, lambda i: (0,) * a.ndim)
    return pl.pallas_call(
        _merge_kernel,
        grid=(n // tm,),
        in_specs=[row(ATTN_WIDTH), row(LRU_WIDTH), row(D_MODEL), row(D_MODEL), row(D_MODEL),
                  full(wao), full(wlo), full(wout), full(g), full(wrh), full(wrl), full(br)],
        out_specs=[row(D_MODEL), pl.BlockSpec((cap, HALF), lambda i: (i, 0)),
                   row(LANES), row(LANES), pl.BlockSpec((8, LANES), lambda i: (i, 0))],
        out_shape=[jax.ShapeDtypeStruct((n, D_MODEL), F32),
                   jax.ShapeDtypeStruct((n // tm * cap, HALF), jnp.uint32),
                   jax.ShapeDtypeStruct((n, LANES), jnp.int32),
                   jax.ShapeDtypeStruct((n, LANES), F32),
                   jax.ShapeDtypeStruct((n // tm * 8, LANES), F32)],
        compiler_params=pltpu.CompilerParams(
            dimension_semantics=("parallel",), vmem_limit_bytes=VMEM_LIMIT),
        name="merge_router",
    )(o, hl, sga, sgl, x2, wao, wlo, wout, g, wrh, wrl, br)


def _expert_kernel(te_ref, nv_ref, na_ref, src_ref, hc_hbm_ref, wgu_ref, bgu_ref, wd_ref, bd_ref,
                   yc_ref, xbuf, ybuf, sem_in, sem_out):
    del te_ref, hc_hbm_ref
    i = pl.program_id(0)
    na = na_ref[0]
    slot = i % 2
    n_chunks = xbuf.shape[1] // CHUNK

    def hbm_rows(tile, j):
        return yc_ref.at[pl.ds(pl.multiple_of(src_ref[tile * n_chunks + j] * CHUNK, CHUNK), CHUNK)]

    def buf_rows(buf, s, j):
        return buf.at[s, pl.ds(pl.multiple_of(j * CHUNK, CHUNK), CHUNK)]

    def in_copy(tile, s, j):
        return pltpu.make_async_copy(hbm_rows(tile, j), buf_rows(xbuf, s, j), sem_in.at[s])

    def out_copy(tile, s, j):
        return pltpu.make_async_copy(buf_rows(ybuf, s, j), hbm_rows(tile, j), sem_out.at[s])

    def start_in(tile, s):
        lax.fori_loop(0, n_chunks, lambda j, c: (in_copy(tile, s, j).start(), c)[1], 0)

    def wait_in(tile, s):
        lax.fori_loop(0, n_chunks, lambda j, c: (in_copy(tile, s, j).wait(), c)[1], 0)

    def start_out(tile, s):
        lax.fori_loop(0, nv_ref[tile], lambda j, c: (out_copy(tile, s, j).start(), c)[1], 0)

    def wait_out(tile, s):
        lax.fori_loop(0, nv_ref[tile], lambda j, c: (out_copy(tile, s, j).wait(), c)[1], 0)

    @pl.when(i == 0)
    def _():
        start_in(0, 0)

    @pl.when(i + 1 < na)
    def _():
        start_in(i + 1, 1 - slot)

    @pl.when(i < na)
    def _():
        wait_in(i, slot)

        @pl.when(i >= 2)
        def _():
            wait_out(i - 2, slot)

        w = xbuf[slot]
        lo = _unpack_row(w, 0).astype(BF16)
        hi = _unpack_row(w, 1).astype(BF16)
        gu = (jnp.dot(lo, wgu_ref[0, :HALF, :], preferred_element_type=F32)
              + jnp.dot(hi, wgu_ref[0, HALF:, :], preferred_element_type=F32)
              + bgu_ref[0])
        g = jnp.minimum(gu[:, :D_EXPERT], SWIGLU_LIMIT)
        u = jnp.clip(gu[:, D_EXPERT:], -SWIGLU_LIMIT, SWIGLU_LIMIT)
        act = (u + 1.0) * (g * jax.nn.sigmoid(SWIGLU_ALPHA * g))
        y = jnp.dot(act.astype(BF16), wd_ref[0], preferred_element_type=F32) + bd_ref[0]
        ybuf[slot] = _pack_row(y)
        start_out(i, slot)

        @pl.when(i == na - 1)
        def _():
            @pl.when(i >= 1)
            def _():
                wait_out(i - 1, 1 - slot)

            wait_out(i, slot)


def _experts(tile_e, tile_nv, n_active, chunk_src, hc, wgu, bgu, wd, bd, tile_rows, n_tiles):
    per_e = lambda a: pl.BlockSpec((1,) + a.shape[1:], lambda i, te, nv, na, src: (te[i], 0, 0))
    return pl.pallas_call(
        _expert_kernel,
        grid_spec=pltpu.PrefetchScalarGridSpec(
            num_scalar_prefetch=4,
            grid=(n_tiles,),
            in_specs=[pl.BlockSpec(memory_space=pl.ANY),
                      per_e(wgu), per_e(bgu), per_e(wd), per_e(bd)],
            out_specs=pl.BlockSpec(memory_space=pl.ANY),
            scratch_shapes=[pltpu.VMEM((2, tile_rows, HALF), jnp.uint32),
                            pltpu.VMEM((2, tile_rows, HALF), jnp.uint32),
                            pltpu.SemaphoreType.DMA((2,)),
                            pltpu.SemaphoreType.DMA((2,))],
        ),
        out_shape=jax.ShapeDtypeStruct(hc.shape, jnp.uint32),
        input_output_aliases={4: 0},
        compiler_params=pltpu.CompilerParams(
            dimension_semantics=("arbitrary",), vmem_limit_bytes=VMEM_LIMIT),
        name="experts",
    )(tile_e, tile_nv, n_active, chunk_src, hc, wgu, bgu, wd, bd)


def _combine_kernel(yc_ref, pos_ref, gate_ref, x1_ref, g_ref, o_ref):
    tm = x1_ref.shape[0]
    cap = yc_ref.shape[0]
    pos = pos_ref[...]
    gate = gate_ref[...]
    slot_id = lax.broadcasted_iota(jnp.int32, (tm, cap), 1)
    sel = jnp.zeros((tm, cap), F32)
    for k in range(TOP_K):
        sel = jnp.where(slot_id == pos[:, k:k + 1], gate[:, k:k + 1], sel)
    sel = sel.astype(BF16)
    w = yc_ref[...]
    lo = jnp.dot(sel, _unpack_row(w, 0).astype(BF16), preferred_element_type=F32)
    hi = jnp.dot(sel, _unpack_row(w, 1).astype(BF16), preferred_element_type=F32)
    x2 = x1_ref[...] + jnp.concatenate([lo, hi], axis=-1)
    o_ref[...] = _rms(x2, g_ref[...])


def _combine(yc, pos, gate, x1, g, tm, cap):
    n = x1.shape[0]
    row = lambda w: pl.BlockSpec((tm, w), lambda i: (i, 0))
    return pl.pallas_call(
        _combine_kernel,
        grid=(n // tm,),
        in_specs=[pl.BlockSpec((cap, HALF), lambda i: (i, 0)), row(LANES), row(LANES),
                  row(D_MODEL), pl.BlockSpec((1, D_MODEL), lambda i: (0, 0))],
        out_specs=row(D_MODEL),
        out_shape=jax.ShapeDtypeStruct((n, D_MODEL), F32),
        compiler_params=pltpu.CompilerParams(
            dimension_semantics=("parallel",), vmem_limit_bytes=VMEM_LIMIT),
        name="combine",
    )(yc, pos, gate, x1, g)


def _block_diag(w):
    eye = jnp.eye(LRU_HEADS, dtype=w.dtype)
    return jnp.einsum("hij,hg->higj", w, eye).reshape(LRU_WIDTH, LRU_WIDTH)


def _pick(n, pref):
    t = pref
    while n % t:
        t //= 2
    return t


def _tile_plan(cnt, tm, cap, tile_rows, n_tiles):
    i32 = jnp.int32
    ntt = cnt.shape[0]
    cpt = tile_rows // CHUNK
    pcc = (cnt + CHUNK - 1) // CHUNK
    seg_base = jnp.arange(ntt, dtype=i32)[:, None] * (cap // CHUNK) + jnp.cumsum(pcc, axis=1) - pcc
    seg_end = jnp.cumsum(pcc, axis=0)
    tot = seg_end[-1]
    ntile = (tot + cpt - 1) // cpt
    tile_end = jnp.cumsum(ntile)
    tile_start = tile_end - ntile
    n_active = tile_end[-1]
    tid = jnp.minimum(jnp.arange(n_tiles, dtype=i32), n_active - 1)
    tile_e = jnp.sum((tile_end[None, :] <= tid[:, None]).astype(i32), axis=1)
    q0 = (tid - tile_start[tile_e]) * cpt
    live = jnp.arange(n_tiles, dtype=i32) < n_active
    tile_nv = jnp.where(live, jnp.clip(tot[tile_e] - q0, 0, cpt), 0).astype(i32)
    q = q0[:, None] + jnp.arange(cpt, dtype=i32)[None, :]
    ends = seg_end.T[tile_e]
    b = jnp.sum((ends[:, None, :] <= q[:, :, None]).astype(i32), axis=2)
    b = jnp.minimum(b, ntt - 1)
    e2 = jnp.broadcast_to(tile_e[:, None], b.shape)
    src = seg_base[b, e2] + q - (seg_end[b, e2] - pcc[b, e2])
    pad_chunk = cap // CHUNK - 1
    src = jnp.where(q < tot[tile_e][:, None], src, pad_chunk)
    return tile_e.astype(i32), tile_nv, n_active.reshape(1).astype(i32), src.reshape(-1).astype(i32)


def _layer(x2, pos2, bsz, seq, norm_mix_g, w_in, conv_w, conv_b, w_rg, b_rg, w_ig, b_ig,
           lru_lambda, attn_sinks, w_attn_o, w_lru_o, w_out, norm_ffn_g, w_router, b_router,
           w_gate_up, b_gate_up, w_down, b_down, norm_out_g):
    n = x2.shape[0]
    tm = _pick(n, TOKEN_TILE)
    row = lambda a: a.reshape(1, -1).astype(F32)

    half = np.arange(0, HEAD_DIM, 2, dtype=np.float32) / HEAD_DIM
    inv_freq = (1.0 / (ROPE_THETA ** jnp.asarray(half, F32)))
    invf = jnp.tile(inv_freq, LANES // (HEAD_DIM // 2)).reshape(1, LANES)

    q, k, v, xr, gy, sga, sgl = _in_proj(x2, pos2, row(norm_mix_g), invf, w_in.astype(BF16), tm)
    o = _attention(attn_sinks.astype(F32), q, k, v, bsz, seq)

    wg = jnp.concatenate([_block_diag(w_rg), _block_diag(w_ig)], axis=1).astype(BF16)
    bg = jnp.concatenate([b_rg, b_ig]).reshape(1, -1).astype(F32)
    hl = _lru(xr, gy, conv_w.astype(F32), row(conv_b), wg, bg, row(lru_lambda),
              bsz, seq, _pick(seq, LRU_TILE))

    cap = tm * TOP_K + N_EXPERTS * CHUNK
    wr = jnp.zeros((D_MODEL, LANES), F32).at[:, :N_EXPERTS].set(w_router.astype(F32))
    wrh = wr.astype(BF16)
    wrl = (wr - wrh.astype(F32)).astype(BF16)
    br = jnp.full((1, LANES), NEG_INF, F32).at[0, :N_EXPERTS].set(b_router.astype(F32))
    x1, hc, pos, gate, cnt = _merge(
        o, hl, sga, sgl, x2, w_attn_o.astype(BF16), w_lru_o.astype(BF16), w_out.astype(BF16),
        row(norm_ffn_g), wrh, wrl, br, tm, cap)

    ntt = n // tm
    cpt = EXPERT_TILE // CHUNK
    max_chunks = ntt * (tm * TOP_K // CHUNK + N_EXPERTS)
    n_tiles = -(-max_chunks // cpt) + N_EXPERTS
    counts = cnt.reshape(ntt, 8, LANES)[:, 0, :N_EXPERTS].astype(jnp.int32)
    tile_e, tile_nv, n_active, chunk_src = _tile_plan(counts, tm, cap, EXPERT_TILE, n_tiles)
    yc = _experts(tile_e, tile_nv, n_active, chunk_src, hc,
                  w_gate_up.astype(BF16), b_gate_up.reshape(N_EXPERTS, 1, -1).astype(F32),
                  w_down.astype(BF16), b_down.reshape(N_EXPERTS, 1, -1).astype(F32),
                  EXPERT_TILE, n_tiles)
    return _combine(yc, pos, gate, x1, row(norm_out_g), tm, cap)


def kernel(x, positions, norm_mix_g, w_in, conv_w, conv_b, w_rg, b_rg, w_ig, b_ig, lru_lambda,
           attn_sinks, w_attn_o, w_lru_o, w_out, norm_ffn_g, w_router, b_router, w_gate_up,
           b_gate_up, w_down, b_down, norm_final_g):
    bsz, seq, d = x.shape
    depth = w_in.shape[0]
    assert depth == 1 and d == D_MODEL and seq % BLOCK == 0
    x2 = x.reshape(bsz * seq, d)
    pos2 = positions.reshape(bsz * seq, 1).astype(jnp.int32)
    out = _layer(x2, pos2, bsz, seq, norm_mix_g[0], w_in[0], conv_w[0], conv_b[0], w_rg[0],
                 b_rg[0], w_ig[0], b_ig[0], lru_lambda[0], attn_sinks[0], w_attn_o[0],
                 w_lru_o[0], w_out[0], norm_ffn_g[0], w_router[0], b_router[0], w_gate_up[0],
                 b_gate_up[0], w_down[0], b_down[0], norm_final_g)
    return out.reshape(bsz, seq, d)
```

```python
import functools

import numpy as np
import jax
import jax.numpy as jnp
from jax import lax
from jax.experimental import pallas as pl
from jax.experimental.pallas import tpu as pltpu

F32 = jnp.float32
BF16 = jnp.bfloat16

D_MODEL = 1024
HEAD_DIM = 64
N_Q_HEADS = 8
N_KV_HEADS = 2
GROUP = N_Q_HEADS // N_KV_HEADS
ATTN_WIDTH = N_Q_HEADS * HEAD_DIM
KV_WIDTH = N_KV_HEADS * HEAD_DIM
WINDOW = 128
BLOCK = 128
ROPE_THETA = 10000.0
LRU_WIDTH = D_MODEL // 2
LRU_HEADS = 8
LRU_BLOCK = LRU_WIDTH // LRU_HEADS
LRU_C = 8.0
CONV_WIDTH = 4
N_EXPERTS = 32
TOP_K = 4
D_EXPERT = D_MODEL
SWIGLU_LIMIT = 7.0
SWIGLU_ALPHA = 1.702
NORM_EPS = 1e-6
NEG_INF = -1e30
IN_SPLITS = (ATTN_WIDTH, KV_WIDTH, KV_WIDTH, LRU_WIDTH, LRU_WIDTH, D_MODEL, D_MODEL)
IN_WIDTH = sum(IN_SPLITS)

LANES = 128
HALF = D_MODEL // 2
VMEM_LIMIT = 56 * 1024 * 1024

TOKEN_TILE = 512
LRU_TILE = 256
EXPERT_TILE = 512
CHUNK = 8
PERM_SPLIT = 4


def _rms(x, g):
    return x * lax.rsqrt(jnp.mean(x * x, axis=-1, keepdims=True) + NORM_EPS) * g


def _pack_row(v):
    return pltpu.pack_elementwise([v[:, :HALF], v[:, HALF:]], packed_dtype=BF16)


def _unpack_row(w, index):
    return pltpu.unpack_elementwise(w, index=index, packed_dtype=BF16, unpacked_dtype=F32)


def _in_proj_kernel(x_ref, pos_ref, g_ref, invf_ref, w_ref,
                    q_ref, k_ref, v_ref, xr_ref, gy_ref, sga_ref, sgl_ref):
    h = _rms(x_ref[...], g_ref[...]).astype(BF16)

    def proj(lo, width):
        return jnp.dot(h, w_ref[:, lo:lo + width], preferred_element_type=F32)

    ang = pos_ref[...].astype(F32) * invf_ref[...]
    cos = jnp.cos(ang)
    sin = jnp.sin(ang)
    lane = lax.broadcasted_iota(jnp.int32, (1, LANES), 1)
    first_half = (lane % HEAD_DIM) < (HEAD_DIM // 2)
    sin_signed = jnp.where(first_half, -sin, sin)

    def rope(t):
        width = t.shape[-1]
        reps = width // LANES
        c = jnp.concatenate([cos] * reps, axis=-1) if reps > 1 else cos
        s = jnp.concatenate([sin_signed] * reps, axis=-1) if reps > 1 else sin_signed
        fh = jnp.concatenate([first_half] * reps, axis=-1) if reps > 1 else first_half
        nxt = pltpu.roll(t, width - HEAD_DIM // 2, axis=1)
        prv = pltpu.roll(t, HEAD_DIM // 2, axis=1)
        return t * c + jnp.where(fh, nxt, prv) * s

    o = 0
    q_ref[...] = (rope(proj(o, ATTN_WIDTH)) * (HEAD_DIM ** -0.5)).astype(BF16)
    o += ATTN_WIDTH
    k_ref[...] = rope(proj(o, KV_WIDTH)).astype(BF16)
    o += KV_WIDTH
    v_ref[...] = proj(o, KV_WIDTH).astype(BF16)
    o += KV_WIDTH
    xr_ref[...] = proj(o, LRU_WIDTH).astype(BF16)
    o += LRU_WIDTH
    gy_ref[...] = jax.nn.gelu(proj(o, LRU_WIDTH)).astype(BF16)
    o += LRU_WIDTH
    sga_ref[...] = jax.nn.sigmoid(proj(o, D_MODEL)).astype(BF16)
    o += D_MODEL
    sgl_ref[...] = jax.nn.sigmoid(proj(o, D_MODEL)).astype(BF16)


def _in_proj(x2, pos2, g, invf, w_in_bf, tm):
    n = x2.shape[0]
    row = lambda w: pl.BlockSpec((tm, w), lambda i: (i, 0))
    full = lambda a: pl.BlockSpec(a.shape, lambda i: (0,) * a.ndim)
    widths = (ATTN_WIDTH, KV_WIDTH, KV_WIDTH, LRU_WIDTH, LRU_WIDTH, D_MODEL, D_MODEL)
    return pl.pallas_call(
        _in_proj_kernel,
        grid=(n // tm,),
        in_specs=[row(D_MODEL), row(1), full(g), full(invf), full(w_in_bf)],
        out_specs=[row(w) for w in widths],
        out_shape=[jax.ShapeDtypeStruct((n, w), BF16) for w in widths],
        compiler_params=pltpu.CompilerParams(
            dimension_semantics=("parallel",), vmem_limit_bytes=VMEM_LIMIT),
        name="in_proj",
    )(x2, pos2, g, invf, w_in_bf)


def _attn_kernel(sink_ref, q_ref, kc_ref, kp_ref, vc_ref, vp_ref, o_ref):
    j = pl.program_id(1)
    q = q_ref[...]
    kk = jnp.concatenate([kp_ref[...], kc_ref[...]], axis=0)
    vv = jnp.concatenate([vp_ref[...], vc_ref[...]], axis=0)
    row = lax.broadcasted_iota(jnp.int32, (BLOCK, 2 * BLOCK), 0)
    col = lax.broadcasted_iota(jnp.int32, (BLOCK, 2 * BLOCK), 1)
    mask = (col > row) & (col <= row + BLOCK) & ((col >= BLOCK) | (j > 0))
    outs = []
    for h in range(N_Q_HEADS):
        g = h // GROUP
        qh = q[:, h * HEAD_DIM:(h + 1) * HEAD_DIM]
        kg = kk[:, g * HEAD_DIM:(g + 1) * HEAD_DIM]
        vg = vv[:, g * HEAD_DIM:(g + 1) * HEAD_DIM]
        s = lax.dot_general(qh, kg, (((1,), (1,)), ((), ())), preferred_element_type=F32)
        s = jnp.where(mask, s, NEG_INF)
        sink = sink_ref[h]
        m = jnp.maximum(jnp.max(s, axis=-1, keepdims=True), sink)
        p = jnp.exp(s - m)
        denom = jnp.sum(p, axis=-1, keepdims=True) + jnp.exp(sink - m)
        oh = jnp.dot(p.astype(BF16), vg, preferred_element_type=F32)
        outs.append(oh / denom)
    o_ref[...] = jnp.concatenate(outs, axis=-1).astype(BF16)


def _attention(sinks, q, k, v, bsz, seq):
    n = q.shape[0]
    nb = seq // BLOCK
    cur = lambda w: pl.BlockSpec((BLOCK, w), lambda b, j: (b * nb + j, 0))
    prev = lambda w: pl.BlockSpec((BLOCK, w), lambda b, j: (b * nb + jnp.maximum(j - 1, 0), 0))
    return pl.pallas_call(
        _attn_kernel,
        grid=(bsz, nb),
        in_specs=[pl.BlockSpec(memory_space=pltpu.SMEM),
                  cur(ATTN_WIDTH), cur(KV_WIDTH), prev(KV_WIDTH), cur(KV_WIDTH), prev(KV_WIDTH)],
        out_specs=cur(ATTN_WIDTH),
        out_shape=jax.ShapeDtypeStruct((n, ATTN_WIDTH), BF16),
        compiler_params=pltpu.CompilerParams(
            dimension_semantics=("parallel", "parallel"), vmem_limit_bytes=VMEM_LIMIT),
        name="attn",
    )(sinks, q, k, k, v, v)


def _lru_kernel(xr_ref, gy_ref, cw_ref, cb_ref, wg_ref, bg_ref, lam_ref, o_ref,
                tail_ref, h_ref):
    ts = xr_ref.shape[0]

    @pl.when(pl.program_id(1) == 0)
    def _():
        tail_ref[...] = jnp.zeros_like(tail_ref)
        h_ref[...] = jnp.zeros_like(h_ref)

    x = xr_ref[...].astype(F32)
    prev = tail_ref[...]
    row8 = lax.broadcasted_iota(jnp.int32, (8, LRU_WIDTH), 0)
    cw = cw_ref[...]
    xc = x * cw[CONV_WIDTH - 1:CONV_WIDTH] + cb_ref[...]
    for d in range(1, CONV_WIDTH):
        xs = pltpu.roll(x, d, axis=0)
        ps = pltpu.roll(prev, d, axis=0)
        head = jnp.where(row8 < d, ps, xs[0:8])
        xs = jnp.concatenate([head, xs[8:]], axis=0)
        xc = xc + xs * cw[CONV_WIDTH - 1 - d:CONV_WIDTH - d]
    tail_ref[...] = x[ts - 8:ts]

    gates = jnp.dot(xc.astype(BF16), wg_ref[...], preferred_element_type=F32) + bg_ref[...]
    r = jax.nn.sigmoid(gates[:, :LRU_WIDTH])
    i = jax.nn.sigmoid(gates[:, LRU_WIDTH:])
    log_a = (-LRU_C) * r * jax.nn.softplus(-lam_ref[...])
    a = jnp.exp(log_a)
    u = jnp.sqrt(1.0 - a * a) * (i * xc)

    rows = lax.broadcasted_iota(jnp.int32, (ts, LRU_WIDTH), 0)
    d = 1
    while d < ts:
        keep = rows >= d
        a_s = jnp.where(keep, pltpu.roll(a, d, axis=0), 1.0)
        u_s = jnp.where(keep, pltpu.roll(u, d, axis=0), 0.0)
        u = a * u_s + u
        a = a * a_s
        d *= 2
    h = a * h_ref[0:1] + u
    h_ref[...] = jnp.broadcast_to(h[ts - 1:ts], h_ref.shape)
    o_ref[...] = (h * gy_ref[...].astype(F32)).astype(BF16)


def _lru(xr, gy, cw, cb, wg, bg, lam, bsz, seq, ts):
    n = xr.shape[0]
    nt = seq // ts
    tile = pl.BlockSpec((ts, LRU_WIDTH), lambda b, j: (b * nt + j, 0))
    full = lambda a: pl.BlockSpec(a.shape, lambda b, j: (0,) * a.ndim)
    return pl.pallas_call(
        _lru_kernel,
        grid=(bsz, nt),
        in_specs=[tile, tile, full(cw), full(cb), full(wg), full(bg), full(lam)],
        out_specs=tile,
        out_shape=jax.ShapeDtypeStruct((n, LRU_WIDTH), BF16),
        scratch_shapes=[pltpu.VMEM((8, LRU_WIDTH), F32), pltpu.VMEM((8, LRU_WIDTH), F32)],
        compiler_params=pltpu.CompilerParams(
            dimension_semantics=("parallel", "arbitrary"), vmem_limit_bytes=VMEM_LIMIT),
        name="lru",
    )(xr, gy, cw, cb, wg, bg, lam)


def _merge_kernel(o_ref, hl_ref, sga_ref, sgl_ref, x_ref, wao_ref, wlo_ref, wout_ref,
                  g_ref, wrh_ref, wrl_ref, br_ref,
                  x1_ref, hc_ref, pos_ref, gate_ref, cnt_ref):
    last = pl.num_programs(0) - 1

    @pl.when(pl.program_id(0) == last)
    def _():
        hc_ref[...] = jnp.zeros_like(hc_ref)

    @pl.when(pl.program_id(0) < last)
    def _():
        _merge_tile(o_ref, hl_ref, sga_ref, sgl_ref, x_ref, wao_ref, wlo_ref, wout_ref,
                    g_ref, wrh_ref, wrl_ref, br_ref, x1_ref, hc_ref, pos_ref, gate_ref, cnt_ref)


def _merge_tile(o_ref, hl_ref, sga_ref, sgl_ref, x_ref, wao_ref, wlo_ref, wout_ref,
                g_ref, wrh_ref, wrl_ref, br_ref,
                x1_ref, hc_ref, pos_ref, gate_ref, cnt_ref):
    tm = x_ref.shape[0]
    cap = hc_ref.shape[0]

    ya = jnp.dot(o_ref[...], wao_ref[...], preferred_element_type=F32)
    yl = jnp.dot(hl_ref[...], wlo_ref[...], preferred_element_type=F32)
    merged = sga_ref[...].astype(F32) * ya + sgl_ref[...].astype(F32) * yl
    x1 = x_ref[...] + jnp.dot(merged.astype(BF16), wout_ref[...], preferred_element_type=F32)
    x1_ref[...] = x1
    h2 = _rms(x1, g_ref[...])
    h2b = h2.astype(BF16)

    h2l = (h2 - h2b.astype(F32)).astype(BF16)
    logits = (jnp.dot(h2b, wrh_ref[...], preferred_element_type=F32)
              + jnp.dot(h2l, wrh_ref[...], preferred_element_type=F32)
              + jnp.dot(h2b, wrl_ref[...], preferred_element_type=F32)
              + br_ref[...])
    lane = lax.broadcasted_iota(jnp.int32, (tm, LANES), 1)
    work = logits
    vals, sels = [], []
    for _ in range(TOP_K):
        m = jnp.max(work, axis=-1, keepdims=True)
        sel = jnp.min(jnp.where(work == m, lane, LANES), axis=-1, keepdims=True)
        vals.append(m)
        sels.append(sel)
        work = jnp.where(lane == sel, -jnp.inf, work)
    exps = [jnp.exp(v - vals[0]) for v in vals]
    tot = exps[0] + exps[1] + exps[2] + exps[3]

    onehot = jnp.zeros((tm, LANES), F32)
    for sel in sels:
        onehot = onehot + (lane == sel).astype(F32)
    r_i = lax.broadcasted_iota(jnp.int32, (tm, tm), 0)
    c_i = lax.broadcasted_iota(jnp.int32, (tm, tm), 1)
    earlier = jnp.dot((c_i < r_i).astype(BF16), onehot.astype(BF16),
                      preferred_element_type=F32)
    cnt = jnp.sum(onehot, axis=0, keepdims=True)
    chunks = jnp.floor((cnt + (CHUNK - 1)) * (1.0 / CHUNK))
    e_r = lax.broadcasted_iota(jnp.int32, (LANES, LANES), 0)
    e_c = lax.broadcasted_iota(jnp.int32, (LANES, LANES), 1)
    start = jnp.dot(jnp.broadcast_to(chunks, (8, LANES)).astype(BF16), (e_r < e_c).astype(BF16),
                    preferred_element_type=F32)[0:1] * float(CHUNK)
    slot_of = earlier + start

    pos_o = jnp.full((tm, LANES), -1.0, F32)
    gate_o = jnp.zeros((tm, LANES), F32)
    for k in range(TOP_K):
        pk = jnp.sum(jnp.where(lane == sels[k], slot_of, 0.0), axis=-1, keepdims=True)
        pos_o = jnp.where(lane == k, pk, pos_o)
        gate_o = jnp.where(lane == k, exps[k] / tot, gate_o)
    pos_ref[...] = pos_o.astype(jnp.int32)
    gate_ref[...] = gate_o
    cnt_ref[...] = jnp.broadcast_to(cnt, cnt_ref.shape)

    pos_t = jnp.transpose(pos_o).astype(jnp.int32)
    rows_per = cap // PERM_SPLIT
    for c in range(PERM_SPLIT):
        slot_id = lax.broadcasted_iota(jnp.int32, (rows_per, tm), 0) + c * rows_per
        perm = jnp.zeros((rows_per, tm), F32)
        for k in range(TOP_K):
            perm = jnp.where(slot_id == pos_t[k:k + 1], 1.0, perm)
        rows = jnp.dot(perm.astype(BF16), h2b, preferred_element_type=F32)
        hc_ref[c * rows_per:(c + 1) * rows_per, :] = _pack_row(rows)


def _merge(o, hl, sga, sgl, x2, wao, wlo, wout, g, wrh, wrl, br, tm, cap):
    n = x2.shape[0]
    ntt = n // tm
    row = lambda w: pl.BlockSpec((tm, w), lambda i: (jnp.minimum(i, ntt - 1), 0))
    full = lambda a: pl.BlockSpec(a.shape, lambda i: (0,) * a.ndim)
    return pl.pallas_call(
        _merge_kernel,
        grid=(ntt + 1,),
        in_specs=[row(ATTN_WIDTH), row(LRU_WIDTH), row(D_MODEL), row(D_MODEL), row(D_MODEL),
                  full(wao), full(wlo), full(wout), full(g), full(wrh), full(wrl), full(br)],
        out_specs=[row(D_MODEL), pl.BlockSpec((cap, HALF), lambda i: (i, 0)),
                   row(LANES), row(LANES),
                   pl.BlockSpec((8, LANES), lambda i: (jnp.minimum(i, ntt - 1), 0))],
        out_shape=[jax.ShapeDtypeStruct((n, D_MODEL), F32),
                   jax.ShapeDtypeStruct(((ntt + 1) * cap, HALF), jnp.uint32),
                   jax.ShapeDtypeStruct((n, LANES), jnp.int32),
                   jax.ShapeDtypeStruct((n, LANES), F32),
                   jax.ShapeDtypeStruct((ntt * 8, LANES), F32)],
        compiler_params=pltpu.CompilerParams(
            dimension_semantics=("arbitrary",), vmem_limit_bytes=VMEM_LIMIT),
        name="merge_router",
    )(o, hl, sga, sgl, x2, wao, wlo, wout, g, wrh, wrl, br)


def _expert_kernel(te_ref, na_ref, src_ref, dst_ref, hc_hbm_ref, wgu_ref, bgu_ref, wd_ref, bd_ref,
                   yc_ref, xb0, xb1, yb0, yb1, wgu_bf, wd_bf, sem_in, sem_out):
    del hc_hbm_ref
    i = pl.program_id(0)
    na = na_ref[0]
    xb = (xb0, xb1)
    yb = (yb0, yb1)
    n_chunks = xb0.shape[0] // CHUNK

    def hbm_rows(chunk):
        return yc_ref.at[pl.ds(pl.multiple_of(chunk * CHUNK, CHUNK), CHUNK)]

    def start_in(tile, s):
        for j in range(n_chunks):
            pltpu.make_async_copy(hbm_rows(src_ref[tile * n_chunks + j]),
                                  xb[s].at[pl.ds(j * CHUNK, CHUNK)], sem_in.at[s]).start()

    def start_out(tile, s):
        for j in range(n_chunks):
            pltpu.make_async_copy(yb[s].at[pl.ds(j * CHUNK, CHUNK)],
                                  hbm_rows(dst_ref[tile * n_chunks + j]), sem_out.at[s]).start()

    def wait_in(s):
        pltpu.make_async_copy(xb[s], xb[s], sem_in.at[s]).wait()

    def wait_out(s):
        pltpu.make_async_copy(yb[s], yb[s], sem_out.at[s]).wait()

    @pl.when(i == 0)
    def _():
        start_in(0, 0)

    @pl.when((i < na) & ((i == 0) | (te_ref[i] != te_ref[jnp.maximum(i - 1, 0)])))
    def _():
        rows = 128

        def cast_gu(r, c):
            sl = pl.ds(pl.multiple_of(r * rows, rows), rows)
            wgu_bf[sl, :] = wgu_ref[0, sl, :].astype(BF16)
            return c

        def cast_d(r, c):
            sl = pl.ds(pl.multiple_of(r * rows, rows), rows)
            wd_bf[sl, :] = wd_ref[0, sl, :].astype(BF16)
            return c

        lax.fori_loop(0, D_MODEL // rows, cast_gu, 0)
        lax.fori_loop(0, D_EXPERT // rows, cast_d, 0)

    def tile_step(s):
        @pl.when(i >= 2)
        def _():
            wait_out(s)

        wait_in(s)
        start_in(i + 1, 1 - s)
        w = xb[s][...]
        lo = _unpack_row(w, 0).astype(BF16)
        hi = _unpack_row(w, 1).astype(BF16)
        gu = (jnp.dot(lo, wgu_bf[:HALF, :], preferred_element_type=F32)
              + jnp.dot(hi, wgu_bf[HALF:, :], preferred_element_type=F32)
              + bgu_ref[0])
        g = jnp.minimum(gu[:, :D_EXPERT], SWIGLU_LIMIT)
        u = jnp.clip(gu[:, D_EXPERT:], -SWIGLU_LIMIT, SWIGLU_LIMIT)
        act = (u + 1.0) * (g * jax.nn.sigmoid(SWIGLU_ALPHA * g))
        y = jnp.dot(act.astype(BF16), wd_bf[...], preferred_element_type=F32) + bd_ref[0]
        yb[s][...] = _pack_row(y)
        start_out(i, s)

        @pl.when(i == na - 1)
        def _():
            wait_in(1 - s)

            @pl.when(i >= 1)
            def _():
                wait_out(1 - s)

            wait_out(s)

    for s in range(2):
        pl.when((i < na) & (i % 2 == s))(functools.partial(tile_step, s))


def _experts(tile_e, n_active, chunk_src, chunk_dst, hc, wgu, bgu, wd, bd, tile_rows, n_tiles):
    per_e = lambda a: pl.BlockSpec((1,) + a.shape[1:], lambda i, te, na, src, dst: (te[i], 0, 0))
    tile_buf = pltpu.VMEM((tile_rows, HALF), jnp.uint32)
    return pl.pallas_call(
        _expert_kernel,
        grid_spec=pltpu.PrefetchScalarGridSpec(
            num_scalar_prefetch=4,
            grid=(n_tiles,),
            in_specs=[pl.BlockSpec(memory_space=pl.ANY),
                      per_e(wgu), per_e(bgu), per_e(wd), per_e(bd)],
            out_specs=pl.BlockSpec(memory_space=pl.ANY),
            scratch_shapes=[tile_buf, tile_buf, tile_buf, tile_buf,
                            pltpu.VMEM(wgu.shape[1:], BF16), pltpu.VMEM(wd.shape[1:], BF16),
                            pltpu.SemaphoreType.DMA((2,)),
                            pltpu.SemaphoreType.DMA((2,))],
        ),
        out_shape=jax.ShapeDtypeStruct(hc.shape, jnp.uint32),
        input_output_aliases={4: 0},
        compiler_params=pltpu.CompilerParams(
            dimension_semantics=("arbitrary",), vmem_limit_bytes=VMEM_LIMIT),
        name="experts",
    )(tile_e, n_active, chunk_src, chunk_dst, hc, wgu, bgu, wd, bd)


def _combine_kernel(yc_ref, pos_ref, gate_ref, x1_ref, g_ref, o_ref):
    tm = x1_ref.shape[0]
    cap = yc_ref.shape[0]
    pos = pos_ref[...]
    gate = gate_ref[...]
    slot_id = lax.broadcasted_iota(jnp.int32, (tm, cap), 1)
    sel = jnp.zeros((tm, cap), F32)
    for k in range(TOP_K):
        sel = jnp.where(slot_id == pos[:, k:k + 1], gate[:, k:k + 1], sel)
    sel = sel.astype(BF16)
    w = yc_ref[...]
    lo = jnp.dot(sel, _unpack_row(w, 0).astype(BF16), preferred_element_type=F32)
    hi = jnp.dot(sel, _unpack_row(w, 1).astype(BF16), preferred_element_type=F32)
    x2 = x1_ref[...] + jnp.concatenate([lo, hi], axis=-1)
    o_ref[...] = _rms(x2, g_ref[...])


def _combine(yc, pos, gate, x1, g, tm, cap):
    n = x1.shape[0]
    row = lambda w: pl.BlockSpec((tm, w), lambda i: (i, 0))
    return pl.pallas_call(
        _combine_kernel,
        grid=(n // tm,),
        in_specs=[pl.BlockSpec((cap, HALF), lambda i: (i, 0)), row(LANES), row(LANES),
                  row(D_MODEL), pl.BlockSpec((1, D_MODEL), lambda i: (0, 0))],
        out_specs=row(D_MODEL),
        out_shape=jax.ShapeDtypeStruct((n, D_MODEL), F32),
        compiler_params=pltpu.CompilerParams(
            dimension_semantics=("parallel",), vmem_limit_bytes=VMEM_LIMIT),
        name="combine",
    )(yc, pos, gate, x1, g)


def _block_diag(w):
    eye = jnp.eye(LRU_HEADS, dtype=w.dtype)
    return jnp.einsum("hij,hg->higj", w, eye).reshape(LRU_WIDTH, LRU_WIDTH)


def _pick(n, pref):
    t = pref
    while n % t:
        t //= 2
    return t


def _tile_plan(cnt, tm, cap, tile_rows, n_tiles):
    i32 = jnp.int32
    ntt = cnt.shape[0]
    cpt = tile_rows // CHUNK
    pcc = (cnt + CHUNK - 1) // CHUNK
    seg_base = jnp.arange(ntt, dtype=i32)[:, None] * (cap // CHUNK) + jnp.cumsum(pcc, axis=1) - pcc
    seg_end = jnp.cumsum(pcc, axis=0)
    seg_start = seg_end - pcc
    tot = seg_end[-1]
    ntile = (tot + cpt - 1) // cpt
    tile_end = jnp.cumsum(ntile)
    tile_start = tile_end - ntile
    n_active = tile_end[-1]
    ids = jnp.arange(n_tiles + 1, dtype=i32)
    live = ids < n_active
    tid = jnp.minimum(ids, n_active - 1)
    tile_e = jnp.sum((tile_end[None, :] <= tid[:, None]).astype(i32), axis=1)
    q = ((tid - tile_start[tile_e]) * cpt)[:, None] + jnp.arange(cpt, dtype=i32)[None, :]
    starts = seg_start.T[tile_e][:, None, :]
    ends = seg_end.T[tile_e][:, None, :]
    shift = (seg_base - seg_start).T[tile_e][:, None, :]
    inside = (starts <= q[:, :, None]) & (q[:, :, None] < ends)
    chunk = q + jnp.sum(jnp.where(inside, shift, 0), axis=2)
    valid = live[:, None] & (q < tot[tile_e][:, None])
    spare = ntt * (cap // CHUNK)
    slot_spare = spare + (ids % 2)[:, None] * cpt + jnp.arange(cpt, dtype=i32)[None, :]
    src = jnp.where(valid, chunk, spare + 2 * cpt)
    dst = jnp.where(valid, chunk, slot_spare)
    return (tile_e[:n_tiles].astype(i32), n_active.reshape(1).astype(i32),
            src.reshape(-1).astype(i32), dst.reshape(-1).astype(i32))


def _layer(x2, pos2, bsz, seq, norm_mix_g, w_in, conv_w, conv_b, w_rg, b_rg, w_ig, b_ig,
           lru_lambda, attn_sinks, w_attn_o, w_lru_o, w_out, norm_ffn_g, w_router, b_router,
           w_gate_up, b_gate_up, w_down, b_down, norm_out_g):
    n = x2.shape[0]
    tm = _pick(n, TOKEN_TILE)
    row = lambda a: a.reshape(1, -1).astype(F32)

    half = np.arange(0, HEAD_DIM, 2, dtype=np.float32) / HEAD_DIM
    inv_freq = (1.0 / (ROPE_THETA ** jnp.asarray(half, F32)))
    invf = jnp.tile(inv_freq, LANES // (HEAD_DIM // 2)).reshape(1, LANES)

    q, k, v, xr, gy, sga, sgl = _in_proj(x2, pos2, row(norm_mix_g), invf, w_in.astype(BF16), tm)
    o = _attention(attn_sinks.astype(F32), q, k, v, bsz, seq)

    wg = jnp.concatenate([_block_diag(w_rg), _block_diag(w_ig)], axis=1).astype(BF16)
    bg = jnp.concatenate([b_rg, b_ig]).reshape(1, -1).astype(F32)
    hl = _lru(xr, gy, conv_w.astype(F32), row(conv_b), wg, bg, row(lru_lambda),
              bsz, seq, _pick(seq, LRU_TILE))

    cap = tm * TOP_K + N_EXPERTS * CHUNK
    wr = jnp.zeros((D_MODEL, LANES), F32).at[:, :N_EXPERTS].set(w_router.astype(F32))
    wrh = wr.astype(BF16)
    wrl = (wr - wrh.astype(F32)).astype(BF16)
    br = jnp.full((1, LANES), NEG_INF, F32).at[0, :N_EXPERTS].set(b_router.astype(F32))
    x1, hc, pos, gate, cnt = _merge(
        o, hl, sga, sgl, x2, w_attn_o.astype(BF16), w_lru_o.astype(BF16), w_out.astype(BF16),
        row(norm_ffn_g), wrh, wrl, br, tm, cap)

    ntt = n // tm
    cpt = EXPERT_TILE // CHUNK
    max_chunks = ntt * (tm * TOP_K // CHUNK + N_EXPERTS)
    n_tiles = -(-max_chunks // cpt) + N_EXPERTS
    counts = cnt.reshape(ntt, 8, LANES)[:, 0, :N_EXPERTS].astype(jnp.int32)
    tile_e, n_active, chunk_src, chunk_dst = _tile_plan(counts, tm, cap, EXPERT_TILE, n_tiles)
    yc = _experts(tile_e, n_active, chunk_src, chunk_dst, hc,
                  w_gate_up.astype(F32), b_gate_up.reshape(N_EXPERTS, 1, -1).astype(F32),
                  w_down.astype(F32), b_down.reshape(N_EXPERTS, 1, -1).astype(F32),
                  EXPERT_TILE, n_tiles)
    return _combine(yc, pos, gate, x1, row(norm_out_g), tm, cap)


def kernel(x, positions, norm_mix_g, w_in, conv_w, conv_b, w_rg, b_rg, w_ig, b_ig, lru_lambda,
           attn_sinks, w_attn_o, w_lru_o, w_out, norm_ffn_g, w_router, b_router, w_gate_up,
           b_gate_up, w_down, b_down, norm_final_g):
    bsz, seq, d = x.shape
    depth = w_in.shape[0]
    assert depth == 1 and d == D_MODEL and seq % BLOCK == 0
    x2 = x.reshape(bsz * seq, d)
    pos2 = positions.reshape(bsz * seq, 1).astype(jnp.int32)
    out = _layer(x2, pos2, bsz, seq, norm_mix_g[0], w_in[0], conv_w[0], conv_b[0], w_rg[0],
                 b_rg[0], w_ig[0], b_ig[0], lru_lambda[0], attn_sinks[0], w_attn_o[0],
                 w_lru_o[0], w_out[0], norm_ffn_g[0], w_router[0], b_router[0], w_gate_up[0],
                 b_gate_up[0], w_down[0], b_down[0], norm_final_g)
    return out.reshape(bsz, seq, d)
```

```python
import functools

import numpy as np
import jax
import jax.numpy as jnp
from jax import lax
from jax.experimental import pallas as pl
from jax.experimental.pallas import tpu as pltpu

F32 = jnp.float32
BF16 = jnp.bfloat16

D_MODEL = 1024
HEAD_DIM = 64
N_Q_HEADS = 8
N_KV_HEADS = 2
GROUP = N_Q_HEADS // N_KV_HEADS
ATTN_WIDTH = N_Q_HEADS * HEAD_DIM
KV_WIDTH = N_KV_HEADS * HEAD_DIM
WINDOW = 128
BLOCK = 128
ROPE_THETA = 10000.0
LRU_WIDTH = D_MODEL // 2
LRU_HEADS = 8
LRU_BLOCK = LRU_WIDTH // LRU_HEADS
LRU_C = 8.0
CONV_WIDTH = 4
N_EXPERTS = 32
TOP_K = 4
D_EXPERT = D_MODEL
SWIGLU_LIMIT = 7.0
SWIGLU_ALPHA = 1.702
NORM_EPS = 1e-6
NEG_INF = -1e30
IN_SPLITS = (ATTN_WIDTH, KV_WIDTH, KV_WIDTH, LRU_WIDTH, LRU_WIDTH, D_MODEL, D_MODEL)
IN_WIDTH = sum(IN_SPLITS)

LANES = 128
SUBLANES = 8
HALF = D_MODEL // 2
VMEM_LIMIT = 56 * 1024 * 1024

TOKEN_TILE = 512
LRU_TILE = 256
EXPERT_TILE = 512
CHUNK = 8
PERM_SPLIT = 4


def _rms(x, g):
    return x * lax.rsqrt(jnp.mean(x * x, axis=-1, keepdims=True) + NORM_EPS) * g


def _sigmoid(z):
    return 0.5 * jnp.tanh(0.5 * z) + 0.5


def _pack_row(v):
    return pltpu.pack_elementwise([v[:, :HALF], v[:, HALF:]], packed_dtype=BF16)


def _unpack_row(w, index):
    return pltpu.unpack_elementwise(w, index=index, packed_dtype=BF16, unpacked_dtype=F32)


def _in_proj_kernel(x_ref, pos_ref, g_ref, invf_ref, w_ref,
                    q_ref, k_ref, v_ref, xr_ref, gy_ref, sga_ref, sgl_ref):
    h = _rms(x_ref[...], g_ref[...]).astype(BF16)

    def proj(lo, width):
        return jnp.dot(h, w_ref[:, lo:lo + width], preferred_element_type=F32)

    ang = pos_ref[...].astype(F32) * invf_ref[...]
    cos = jnp.cos(ang)
    sin = jnp.sin(ang)
    lane = lax.broadcasted_iota(jnp.int32, (1, LANES), 1)
    first_half = (lane % HEAD_DIM) < (HEAD_DIM // 2)
    sin_signed = jnp.where(first_half, -sin, sin)

    def rope(t):
        width = t.shape[-1]
        reps = width // LANES
        c = jnp.concatenate([cos] * reps, axis=-1) if reps > 1 else cos
        s = jnp.concatenate([sin_signed] * reps, axis=-1) if reps > 1 else sin_signed
        fh = jnp.concatenate([first_half] * reps, axis=-1) if reps > 1 else first_half
        nxt = pltpu.roll(t, width - HEAD_DIM // 2, axis=1)
        prv = pltpu.roll(t, HEAD_DIM // 2, axis=1)
        return t * c + jnp.where(fh, nxt, prv) * s

    o = 0
    q_ref[...] = (rope(proj(o, ATTN_WIDTH)) * (HEAD_DIM ** -0.5)).astype(BF16)
    o += ATTN_WIDTH
    k_ref[...] = rope(proj(o, KV_WIDTH)).astype(BF16)
    o += KV_WIDTH
    v_ref[...] = proj(o, KV_WIDTH).astype(BF16)
    o += KV_WIDTH
    xr_ref[...] = proj(o, LRU_WIDTH).astype(BF16)
    o += LRU_WIDTH
    gy_ref[...] = jax.nn.gelu(proj(o, LRU_WIDTH)).astype(BF16)
    o += LRU_WIDTH
    sga_ref[...] = _sigmoid(proj(o, D_MODEL)).astype(BF16)
    o += D_MODEL
    sgl_ref[...] = _sigmoid(proj(o, D_MODEL)).astype(BF16)


def _in_proj(x2, pos2, g, invf, w_in_bf, tm):
    n = x2.shape[0]
    row = lambda w: pl.BlockSpec((tm, w), lambda i: (i, 0))
    full = lambda a: pl.BlockSpec(a.shape, lambda i: (0,) * a.ndim)
    widths = (ATTN_WIDTH, KV_WIDTH, KV_WIDTH, LRU_WIDTH, LRU_WIDTH, D_MODEL, D_MODEL)
    return pl.pallas_call(
        _in_proj_kernel,
        grid=(n // tm,),
        in_specs=[row(D_MODEL), row(1), full(g), full(invf), full(w_in_bf)],
        out_specs=[row(w) for w in widths],
        out_shape=[jax.ShapeDtypeStruct((n, w), BF16) for w in widths],
        compiler_params=pltpu.CompilerParams(
            dimension_semantics=("parallel",), vmem_limit_bytes=VMEM_LIMIT),
        name="in_proj",
    )(x2, pos2, g, invf, w_in_bf)


def _attn_kernel(sink_ref, q_ref, kc_ref, kp_ref, vc_ref, vp_ref, o_ref):
    j = pl.program_id(1)
    q = q_ref[...]
    kk = jnp.concatenate([kp_ref[...], kc_ref[...]], axis=0)
    vv = jnp.concatenate([vp_ref[...], vc_ref[...]], axis=0)
    row = lax.broadcasted_iota(jnp.int32, (BLOCK, 2 * BLOCK), 0)
    col = lax.broadcasted_iota(jnp.int32, (BLOCK, 2 * BLOCK), 1)
    mask = (col > row) & (col <= row + BLOCK) & ((col >= BLOCK) | (j > 0))
    outs = []
    for h in range(N_Q_HEADS):
        g = h // GROUP
        qh = q[:, h * HEAD_DIM:(h + 1) * HEAD_DIM]
        kg = kk[:, g * HEAD_DIM:(g + 1) * HEAD_DIM]
        vg = vv[:, g * HEAD_DIM:(g + 1) * HEAD_DIM]
        s = lax.dot_general(qh, kg, (((1,), (1,)), ((), ())), preferred_element_type=F32)
        s = jnp.where(mask, s, NEG_INF)
        sink = sink_ref[h]
        m = jnp.maximum(jnp.max(s, axis=-1, keepdims=True), sink)
        p = jnp.exp(s - m)
        denom = jnp.sum(p, axis=-1, keepdims=True) + jnp.exp(sink - m)
        oh = jnp.dot(p.astype(BF16), vg, preferred_element_type=F32)
        outs.append(oh / denom)
    o_ref[...] = jnp.concatenate(outs, axis=-1).astype(BF16)


def _attention(sinks, q, k, v, bsz, seq):
    n = q.shape[0]
    nb = seq // BLOCK
    cur = lambda w: pl.BlockSpec((BLOCK, w), lambda b, j: (b * nb + j, 0))
    prev = lambda w: pl.BlockSpec((BLOCK, w), lambda b, j: (b * nb + jnp.maximum(j - 1, 0), 0))
    return pl.pallas_call(
        _attn_kernel,
        grid=(bsz, nb),
        in_specs=[pl.BlockSpec(memory_space=pltpu.SMEM),
                  cur(ATTN_WIDTH), cur(KV_WIDTH), prev(KV_WIDTH), cur(KV_WIDTH), prev(KV_WIDTH)],
        out_specs=cur(ATTN_WIDTH),
        out_shape=jax.ShapeDtypeStruct((n, ATTN_WIDTH), BF16),
        compiler_params=pltpu.CompilerParams(
            dimension_semantics=("parallel", "parallel"), vmem_limit_bytes=VMEM_LIMIT),
        name="attn",
    )(sinks, q, k, k, v, v)


def _lru_kernel(xr_ref, gy_ref, cw_ref, cb_ref, wg_ref, bg_ref, lam_ref, o_ref,
                tail_ref, h_ref):
    ts = xr_ref.shape[0]

    @pl.when(pl.program_id(1) == 0)
    def _():
        tail_ref[...] = jnp.zeros_like(tail_ref)
        h_ref[...] = jnp.zeros_like(h_ref)

    x = xr_ref[...].astype(F32)
    prev = tail_ref[...]
    row8 = lax.broadcasted_iota(jnp.int32, (8, LRU_WIDTH), 0)
    cw = cw_ref[...]
    xc = x * cw[CONV_WIDTH - 1:CONV_WIDTH] + cb_ref[...]
    for d in range(1, CONV_WIDTH):
        xs = pltpu.roll(x, d, axis=0)
        ps = pltpu.roll(prev, d, axis=0)
        head = jnp.where(row8 < d, ps, xs[0:8])
        xs = jnp.concatenate([head, xs[8:]], axis=0)
        xc = xc + xs * cw[CONV_WIDTH - 1 - d:CONV_WIDTH - d]
    tail_ref[...] = x[ts - 8:ts]

    gates = jnp.dot(xc.astype(BF16), wg_ref[...], preferred_element_type=F32) + bg_ref[...]
    r = _sigmoid(gates[:, :LRU_WIDTH])
    i = _sigmoid(gates[:, LRU_WIDTH:])
    log_a = (-LRU_C) * r * jax.nn.softplus(-lam_ref[...])
    a = jnp.exp(log_a)
    u = jnp.sqrt(1.0 - a * a) * (i * xc)

    groups = ts // SUBLANES
    a = a.reshape(groups, SUBLANES, LRU_WIDTH)
    u = u.reshape(groups, SUBLANES, LRU_WIDTH)
    sub = lax.broadcasted_iota(jnp.int32, (groups, SUBLANES, LRU_WIDTH), 1)
    d = 1
    while d < SUBLANES:
        keep = sub >= d
        a_s = jnp.where(keep, pltpu.roll(a, d, axis=1), 1.0)
        u_s = jnp.where(keep, pltpu.roll(u, d, axis=1), 0.0)
        u = a * u_s + u
        a = a * a_s
        d *= 2
    carry = h_ref[0:1]
    hs = []
    for g in range(groups):
        hg = a[g] * carry + u[g]
        hs.append(hg)
        carry = hg[SUBLANES - 1:SUBLANES]
    h_ref[...] = jnp.broadcast_to(carry, h_ref.shape)
    h = jnp.concatenate(hs, axis=0)
    o_ref[...] = (h * gy_ref[...].astype(F32)).astype(BF16)


def _lru(xr, gy, cw, cb, wg, bg, lam, bsz, seq, ts):
    n = xr.shape[0]
    nt = seq // ts
    tile = pl.BlockSpec((ts, LRU_WIDTH), lambda b, j: (b * nt + j, 0))
    full = lambda a: pl.BlockSpec(a.shape, lambda b, j: (0,) * a.ndim)
    return pl.pallas_call(
        _lru_kernel,
        grid=(bsz, nt),
        in_specs=[tile, tile, full(cw), full(cb), full(wg), full(bg), full(lam)],
        out_specs=tile,
        out_shape=jax.ShapeDtypeStruct((n, LRU_WIDTH), BF16),
        scratch_shapes=[pltpu.VMEM((8, LRU_WIDTH), F32), pltpu.VMEM((8, LRU_WIDTH), F32)],
        compiler_params=pltpu.CompilerParams(
            dimension_semantics=("parallel", "arbitrary"), vmem_limit_bytes=VMEM_LIMIT),
        name="lru",
    )(xr, gy, cw, cb, wg, bg, lam)


def _merge_kernel(o_ref, hl_ref, sga_ref, sgl_ref, x_ref, wao_ref, wlo_ref, wout_ref,
                  g_ref, wrh_ref, wrl_ref, br_ref,
                  x1_ref, hc_ref, pos_ref, gate_ref, cnt_ref):
    last = pl.num_programs(0) - 1

    @pl.when(pl.program_id(0) == last)
    def _():
        hc_ref[...] = jnp.zeros_like(hc_ref)

    @pl.when(pl.program_id(0) < last)
    def _():
        _merge_tile(o_ref, hl_ref, sga_ref, sgl_ref, x_ref, wao_ref, wlo_ref, wout_ref,
                    g_ref, wrh_ref, wrl_ref, br_ref, x1_ref, hc_ref, pos_ref, gate_ref, cnt_ref)


def _merge_tile(o_ref, hl_ref, sga_ref, sgl_ref, x_ref, wao_ref, wlo_ref, wout_ref,
                g_ref, wrh_ref, wrl_ref, br_ref,
                x1_ref, hc_ref, pos_ref, gate_ref, cnt_ref):
    tm = x_ref.shape[0]
    cap = hc_ref.shape[0]

    ya = jnp.dot(o_ref[...], wao_ref[...], preferred_element_type=F32)
    yl = jnp.dot(hl_ref[...], wlo_ref[...], preferred_element_type=F32)
    merged = sga_ref[...].astype(F32) * ya + sgl_ref[...].astype(F32) * yl
    x1 = x_ref[...] + jnp.dot(merged.astype(BF16), wout_ref[...], preferred_element_type=F32)
    x1_ref[...] = x1
    h2 = _rms(x1, g_ref[...])
    h2b = h2.astype(BF16)

    h2l = (h2 - h2b.astype(F32)).astype(BF16)
    logits = (jnp.dot(h2b, wrh_ref[...], preferred_element_type=F32)
              + jnp.dot(h2l, wrh_ref[...], preferred_element_type=F32)
              + jnp.dot(h2b, wrl_ref[...], preferred_element_type=F32)
              + br_ref[...])
    lane = lax.broadcasted_iota(jnp.int32, (tm, LANES), 1)
    work = logits
    vals, sels = [], []
    for _ in range(TOP_K):
        m = jnp.max(work, axis=-1, keepdims=True)
        sel = jnp.min(jnp.where(work == m, lane, LANES), axis=-1, keepdims=True)
        vals.append(m)
        sels.append(sel)
        work = jnp.where(lane == sel, -jnp.inf, work)
    exps = [jnp.exp(v - vals[0]) for v in vals]
    tot = exps[0] + exps[1] + exps[2] + exps[3]

    onehot = jnp.zeros((tm, LANES), F32)
    for sel in sels:
        onehot = onehot + (lane == sel).astype(F32)
    r_i = lax.broadcasted_iota(jnp.int32, (tm, tm), 0)
    c_i = lax.broadcasted_iota(jnp.int32, (tm, tm), 1)
    earlier = jnp.dot((c_i < r_i).astype(BF16), onehot.astype(BF16),
                      preferred_element_type=F32)
    cnt = jnp.sum(onehot, axis=0, keepdims=True)
    chunks = jnp.floor((cnt + (CHUNK - 1)) * (1.0 / CHUNK))
    e_r = lax.broadcasted_iota(jnp.int32, (LANES, LANES), 0)
    e_c = lax.broadcasted_iota(jnp.int32, (LANES, LANES), 1)
    start = jnp.dot(jnp.broadcast_to(chunks, (8, LANES)).astype(BF16), (e_r < e_c).astype(BF16),
                    preferred_element_type=F32)[0:1] * float(CHUNK)
    slot_of = earlier + start

    pos_o = jnp.full((tm, LANES), -1.0, F32)
    gate_o = jnp.zeros((tm, LANES), F32)
    for k in range(TOP_K):
        pk = jnp.sum(jnp.where(lane == sels[k], slot_of, 0.0), axis=-1, keepdims=True)
        pos_o = jnp.where(lane == k, pk, pos_o)
        gate_o = jnp.where(lane == k, exps[k] / tot, gate_o)
    pos_ref[...] = pos_o.astype(jnp.int32)
    gate_ref[...] = gate_o
    cnt_ref[...] = jnp.broadcast_to(cnt, cnt_ref.shape)

    pos_t = jnp.transpose(pos_o).astype(jnp.int32)
    rows_per = cap // PERM_SPLIT
    for c in range(PERM_SPLIT):
        slot_id = lax.broadcasted_iota(jnp.int32, (rows_per, tm), 0) + c * rows_per
        perm = jnp.zeros((rows_per, tm), F32)
        for k in range(TOP_K):
            perm = jnp.where(slot_id == pos_t[k:k + 1], 1.0, perm)
        rows = jnp.dot(perm.astype(BF16), h2b, preferred_element_type=F32)
        hc_ref[c * rows_per:(c + 1) * rows_per, :] = _pack_row(rows)


def _merge(o, hl, sga, sgl, x2, wao, wlo, wout, g, wrh, wrl, br, tm, cap):
    n = x2.shape[0]
    ntt = n // tm
    row = lambda w: pl.BlockSpec((tm, w), lambda i: (jnp.minimum(i, ntt - 1), 0))
    full = lambda a: pl.BlockSpec(a.shape, lambda i: (0,) * a.ndim)
    return pl.pallas_call(
        _merge_kernel,
        grid=(ntt + 1,),
        in_specs=[row(ATTN_WIDTH), row(LRU_WIDTH), row(D_MODEL), row(D_MODEL), row(D_MODEL),
                  full(wao), full(wlo), full(wout), full(g), full(wrh), full(wrl), full(br)],
        out_specs=[row(D_MODEL), pl.BlockSpec((cap, HALF), lambda i: (i, 0)),
                   row(LANES), row(LANES),
                   pl.BlockSpec((8, LANES), lambda i: (jnp.minimum(i, ntt - 1), 0))],
        out_shape=[jax.ShapeDtypeStruct((n, D_MODEL), F32),
                   jax.ShapeDtypeStruct(((ntt + 1) * cap, HALF), jnp.uint32),
                   jax.ShapeDtypeStruct((n, LANES), jnp.int32),
                   jax.ShapeDtypeStruct((n, LANES), F32),
                   jax.ShapeDtypeStruct((ntt * 8, LANES), F32)],
        compiler_params=pltpu.CompilerParams(
            dimension_semantics=("arbitrary",), vmem_limit_bytes=VMEM_LIMIT),
        name="merge_router",
    )(o, hl, sga, sgl, x2, wao, wlo, wout, g, wrh, wrl, br)


def _expert_kernel(te_ref, na_ref, src_ref, dst_ref, hc_hbm_ref, wgu_ref, bgu_ref, wd_ref, bd_ref,
                   yc_ref, xb0, xb1, yb0, yb1, wgu_bf, wd_bf, sem_in, sem_out):
    del hc_hbm_ref
    i = pl.program_id(0)
    na = na_ref[0]
    xb = (xb0, xb1)
    yb = (yb0, yb1)
    n_chunks = xb0.shape[0] // CHUNK

    def hbm_rows(chunk):
        return yc_ref.at[pl.ds(pl.multiple_of(chunk * CHUNK, CHUNK), CHUNK)]

    def start_in(tile, s):
        for j in range(n_chunks):
            pltpu.make_async_copy(hbm_rows(src_ref[tile * n_chunks + j]),
                                  xb[s].at[pl.ds(j * CHUNK, CHUNK)], sem_in.at[s]).start()

    def start_out(tile, s):
        for j in range(n_chunks):
            pltpu.make_async_copy(yb[s].at[pl.ds(j * CHUNK, CHUNK)],
                                  hbm_rows(dst_ref[tile * n_chunks + j]), sem_out.at[s]).start()

    def wait_in(s):
        pltpu.make_async_copy(xb[s], xb[s], sem_in.at[s]).wait()

    def wait_out(s):
        pltpu.make_async_copy(yb[s], yb[s], sem_out.at[s]).wait()

    @pl.when(i == 0)
    def _():
        start_in(0, 0)

    @pl.when((i < na) & ((i == 0) | (te_ref[i] != te_ref[jnp.maximum(i - 1, 0)])))
    def _():
        rows = 128

        def cast_gu(r, c):
            sl = pl.ds(pl.multiple_of(r * rows, rows), rows)
            wgu_bf[sl, :] = wgu_ref[0, sl, :].astype(BF16)
            return c

        def cast_d(r, c):
            sl = pl.ds(pl.multiple_of(r * rows, rows), rows)
            wd_bf[sl, :] = wd_ref[0, sl, :].astype(BF16)
            return c

        lax.fori_loop(0, D_MODEL // rows, cast_gu, 0)
        lax.fori_loop(0, D_EXPERT // rows, cast_d, 0)

    def tile_step(s):
        @pl.when(i >= 2)
        def _():
            wait_out(s)

        @pl.when(i + 1 < na)
        def _():
            start_in(i + 1, 1 - s)

        wait_in(s)
        w = xb[s][...]
        lo = _unpack_row(w, 0).astype(BF16)
        hi = _unpack_row(w, 1).astype(BF16)
        gu = (jnp.dot(lo, wgu_bf[:HALF, :], preferred_element_type=F32)
              + jnp.dot(hi, wgu_bf[HALF:, :], preferred_element_type=F32)
              + bgu_ref[0])
        g = jnp.minimum(gu[:, :D_EXPERT], SWIGLU_LIMIT)
        u = jnp.clip(gu[:, D_EXPERT:], -SWIGLU_LIMIT, SWIGLU_LIMIT)
        act = (u + 1.0) * (g * jax.nn.sigmoid(SWIGLU_ALPHA * g))
        y = jnp.dot(act.astype(BF16), wd_bf[...], preferred_element_type=F32) + bd_ref[0]
        yb[s][...] = _pack_row(y)
        start_out(i, s)

        @pl.when(i == na - 1)
        def _():
            @pl.when(i >= 1)
            def _():
                wait_out(1 - s)

            wait_out(s)

    for s in range(2):
        pl.when((i < na) & (i % 2 == s))(functools.partial(tile_step, s))


def _experts(tile_e, n_active, chunk_src, chunk_dst, hc, wgu, bgu, wd, bd, tile_rows, n_tiles):
    per_e = lambda a: pl.BlockSpec((1,) + a.shape[1:], lambda i, te, na, src, dst: (te[i], 0, 0))
    tile_buf = pltpu.VMEM((tile_rows, HALF), jnp.uint32)
    return pl.pallas_call(
        _expert_kernel,
        grid_spec=pltpu.PrefetchScalarGridSpec(
            num_scalar_prefetch=4,
            grid=(n_tiles,),
            in_specs=[pl.BlockSpec(memory_space=pl.ANY),
                      per_e(wgu), per_e(bgu), per_e(wd), per_e(bd)],
            out_specs=pl.BlockSpec(memory_space=pl.ANY),
            scratch_shapes=[tile_buf, tile_buf, tile_buf, tile_buf,
                            pltpu.VMEM(wgu.shape[1:], BF16), pltpu.VMEM(wd.shape[1:], BF16),
                            pltpu.SemaphoreType.DMA((2,)),
                            pltpu.SemaphoreType.DMA((2,))],
        ),
        out_shape=jax.ShapeDtypeStruct(hc.shape, jnp.uint32),
        input_output_aliases={4: 0},
        compiler_params=pltpu.CompilerParams(
            dimension_semantics=("arbitrary",), vmem_limit_bytes=VMEM_LIMIT),
        name="experts",
    )(tile_e, n_active, chunk_src, chunk_dst, hc, wgu, bgu, wd, bd)


def _combine_kernel(yc_ref, pos_ref, gate_ref, x1_ref, g_ref, o_ref):
    tm = x1_ref.shape[0]
    cap = yc_ref.shape[0]
    pos = pos_ref[...]
    gate = gate_ref[...]
    slot_id = lax.broadcasted_iota(jnp.int32, (tm, cap), 1)
    sel = jnp.zeros((tm, cap), F32)
    for k in range(TOP_K):
        sel = jnp.where(slot_id == pos[:, k:k + 1], gate[:, k:k + 1], sel)
    sel = sel.astype(BF16)
    w = yc_ref[...]
    lo = jnp.dot(sel, _unpack_row(w, 0).astype(BF16), preferred_element_type=F32)
    hi = jnp.dot(sel, _unpack_row(w, 1).astype(BF16), preferred_element_type=F32)
    x2 = x1_ref[...] + jnp.concatenate([lo, hi], axis=-1)
    o_ref[...] = _rms(x2, g_ref[...])


def _combine(yc, pos, gate, x1, g, tm, cap):
    n = x1.shape[0]
    row = lambda w: pl.BlockSpec((tm, w), lambda i: (i, 0))
    return pl.pallas_call(
        _combine_kernel,
        grid=(n // tm,),
        in_specs=[pl.BlockSpec((cap, HALF), lambda i: (i, 0)), row(LANES), row(LANES),
                  row(D_MODEL), pl.BlockSpec((1, D_MODEL), lambda i: (0, 0))],
        out_specs=row(D_MODEL),
        out_shape=jax.ShapeDtypeStruct((n, D_MODEL), F32),
        compiler_params=pltpu.CompilerParams(
            dimension_semantics=("parallel",), vmem_limit_bytes=VMEM_LIMIT),
        name="combine",
    )(yc, pos, gate, x1, g)


def _block_diag(w):
    eye = jnp.eye(LRU_HEADS, dtype=w.dtype)
    return jnp.einsum("hij,hg->higj", w, eye).reshape(LRU_WIDTH, LRU_WIDTH)


def _pick(n, pref):
    t = pref
    while n % t:
        t //= 2
    return t


def _tile_plan(cnt, tm, cap, tile_rows, n_tiles):
    i32 = jnp.int32
    ntt = cnt.shape[0]
    cpt = tile_rows // CHUNK
    pcc = (cnt + CHUNK - 1) // CHUNK
    seg_base = jnp.arange(ntt, dtype=i32)[:, None] * (cap // CHUNK) + jnp.cumsum(pcc, axis=1) - pcc
    seg_end = jnp.cumsum(pcc, axis=0)
    seg_start = seg_end - pcc
    tot = seg_end[-1]
    ntile = (tot + cpt - 1) // cpt
    tile_end = jnp.cumsum(ntile)
    tile_start = tile_end - ntile
    n_active = tile_end[-1]
    ids = jnp.arange(n_tiles + 1, dtype=i32)
    live = ids < n_active
    tid = jnp.minimum(ids, n_active - 1)
    tile_e = jnp.sum((tile_end[None, :] <= tid[:, None]).astype(i32), axis=1)
    q = ((tid - tile_start[tile_e]) * cpt)[:, None] + jnp.arange(cpt, dtype=i32)[None, :]
    starts = seg_start.T[tile_e][:, None, :]
    ends = seg_end.T[tile_e][:, None, :]
    shift = (seg_base - seg_start).T[tile_e][:, None, :]
    inside = (starts <= q[:, :, None]) & (q[:, :, None] < ends)
    chunk = q + jnp.sum(jnp.where(inside, shift, 0), axis=2)
    valid = live[:, None] & (q < tot[tile_e][:, None])
    spare = ntt * (cap // CHUNK)
    slot_spare = spare + (ids % 2)[:, None] * cpt + jnp.arange(cpt, dtype=i32)[None, :]
    src = jnp.where(valid, chunk, spare + 2 * cpt)
    dst = jnp.where(valid, chunk, slot_spare)
    return (tile_e[:n_tiles].astype(i32), n_active.reshape(1).astype(i32),
            src.reshape(-1).astype(i32), dst.reshape(-1).astype(i32))


def _layer(x2, pos2, bsz, seq, norm_mix_g, w_in, conv_w, conv_b, w_rg, b_rg, w_ig, b_ig,
           lru_lambda, attn_sinks, w_attn_o, w_lru_o, w_out, norm_ffn_g, w_router, b_router,
           w_gate_up, b_gate_up, w_down, b_down, norm_out_g):
    n = x2.shape[0]
    tm = _pick(n, TOKEN_TILE)
    row = lambda a: a.reshape(1, -1).astype(F32)

    half = np.arange(0, HEAD_DIM, 2, dtype=np.float32) / HEAD_DIM
    inv_freq = (1.0 / (ROPE_THETA ** jnp.asarray(half, F32)))
    invf = jnp.tile(inv_freq, LANES // (HEAD_DIM // 2)).reshape(1, LANES)

    q, k, v, xr, gy, sga, sgl = _in_proj(x2, pos2, row(norm_mix_g), invf, w_in.astype(BF16), tm)
    o = _attention(attn_sinks.astype(F32), q, k, v, bsz, seq)

    wg = jnp.concatenate([_block_diag(w_rg), _block_diag(w_ig)], axis=1).astype(BF16)
    bg = jnp.concatenate([b_rg, b_ig]).reshape(1, -1).astype(F32)
    hl = _lru(xr, gy, conv_w.astype(F32), row(conv_b), wg, bg, row(lru_lambda),
              bsz, seq, _pick(seq, LRU_TILE))

    cap = tm * TOP_K + N_EXPERTS * CHUNK
    wr = jnp.zeros((D_MODEL, LANES), F32).at[:, :N_EXPERTS].set(w_router.astype(F32))
    wrh = wr.astype(BF16)
    wrl = (wr - wrh.astype(F32)).astype(BF16)
    br = jnp.full((1, LANES), NEG_INF, F32).at[0, :N_EXPERTS].set(b_router.astype(F32))
    x1, hc, pos, gate, cnt = _merge(
        o, hl, sga, sgl, x2, w_attn_o.astype(BF16), w_lru_o.astype(BF16), w_out.astype(BF16),
        row(norm_ffn_g), wrh, wrl, br, tm, cap)

    ntt = n // tm
    cpt = EXPERT_TILE // CHUNK
    max_chunks = ntt * (tm * TOP_K // CHUNK + N_EXPERTS)
    n_tiles = -(-max_chunks // cpt) + N_EXPERTS
    counts = cnt.reshape(ntt, 8, LANES)[:, 0, :N_EXPERTS].astype(jnp.int32)
    tile_e, n_active, chunk_src, chunk_dst = _tile_plan(counts, tm, cap, EXPERT_TILE, n_tiles)
    yc = _experts(tile_e, n_active, chunk_src, chunk_dst, hc,
                  w_gate_up.astype(F32), b_gate_up.reshape(N_EXPERTS, 1, -1).astype(F32),
                  w_down.astype(F32), b_down.reshape(N_EXPERTS, 1, -1).astype(F32),
                  EXPERT_TILE, n_tiles)
    return _combine(yc, pos, gate, x1, row(norm_out_g), tm, cap)


def kernel(x, positions, norm_mix_g, w_in, conv_w, conv_b, w_rg, b_rg, w_ig, b_ig, lru_lambda,
           attn_sinks, w_attn_o, w_lru_o, w_out, norm_ffn_g, w_router, b_router, w_gate_up,
           b_gate_up, w_down, b_down, norm_final_g):
    bsz, seq, d = x.shape
    depth = w_in.shape[0]
    assert depth == 1 and d == D_MODEL and seq % BLOCK == 0
    x2 = x.reshape(bsz * seq, d)
    pos2 = positions.reshape(bsz * seq, 1).astype(jnp.int32)
    out = _layer(x2, pos2, bsz, seq, norm_mix_g[0], w_in[0], conv_w[0], conv_b[0], w_rg[0],
                 b_rg[0], w_ig[0], b_ig[0], lru_lambda[0], attn_sinks[0], w_attn_o[0],
                 w_lru_o[0], w_out[0], norm_ffn_g[0], w_router[0], b_router[0], w_gate_up[0],
                 b_gate_up[0], w_down[0], b_down[0], norm_final_g)
    return out.reshape(bsz, seq, d)
```

```python
import functools

import numpy as np
import jax
import jax.numpy as jnp
from jax import lax
from jax.experimental import pallas as pl
from jax.experimental.pallas import tpu as pltpu

F32 = jnp.float32
BF16 = jnp.bfloat16

D_MODEL = 1024
HEAD_DIM = 64
N_Q_HEADS = 8
N_KV_HEADS = 2
GROUP = N_Q_HEADS // N_KV_HEADS
ATTN_WIDTH = N_Q_HEADS * HEAD_DIM
KV_WIDTH = N_KV_HEADS * HEAD_DIM
WINDOW = 128
BLOCK = 128
ROPE_THETA = 10000.0
LRU_WIDTH = D_MODEL // 2
LRU_HEADS = 8
LRU_BLOCK = LRU_WIDTH // LRU_HEADS
LRU_C = 8.0
CONV_WIDTH = 4
N_EXPERTS = 32
TOP_K = 4
D_EXPERT = D_MODEL
SWIGLU_LIMIT = 7.0
SWIGLU_ALPHA = 1.702
NORM_EPS = 1e-6
NEG_INF = -1e30
IN_SPLITS = (ATTN_WIDTH, KV_WIDTH, KV_WIDTH, LRU_WIDTH, LRU_WIDTH, D_MODEL, D_MODEL)
IN_WIDTH = sum(IN_SPLITS)

LANES = 128
SUBLANES = 8
HALF = D_MODEL // 2
VMEM_LIMIT = 56 * 1024 * 1024

TOKEN_TILE = 512
ATTN_TILE = 128
LRU_TILE = 512
EXPERT_TILE = 512
CHUNK = 8
PERM_SPLIT = 4


def _rms(x, g):
    return x * lax.rsqrt(jnp.mean(x * x, axis=-1, keepdims=True) + NORM_EPS) * g


def _sigmoid(z):
    return 0.5 * jnp.tanh(0.5 * z) + 0.5


def _pack_row(v):
    return pltpu.pack_elementwise([v[:, :HALF], v[:, HALF:]], packed_dtype=BF16)


def _unpack_row(w, index):
    return pltpu.unpack_elementwise(w, index=index, packed_dtype=BF16, unpacked_dtype=F32)


def _in_proj_kernel(x_ref, pos_ref, g_ref, invf_ref, w_ref,
                    q_ref, k_ref, v_ref, xr_ref, gy_ref, sga_ref, sgl_ref):
    h = _rms(x_ref[...], g_ref[...]).astype(BF16)

    def proj(lo, width):
        return jnp.dot(h, w_ref[:, lo:lo + width], preferred_element_type=F32)

    ang = pos_ref[...].astype(F32) * invf_ref[...]
    cos = jnp.cos(ang)
    sin = jnp.sin(ang)
    lane = lax.broadcasted_iota(jnp.int32, (1, LANES), 1)
    first_half = (lane % HEAD_DIM) < (HEAD_DIM // 2)
    sin_signed = jnp.where(first_half, -sin, sin)

    def rope(t):
        width = t.shape[-1]
        reps = width // LANES
        c = jnp.concatenate([cos] * reps, axis=-1) if reps > 1 else cos
        s = jnp.concatenate([sin_signed] * reps, axis=-1) if reps > 1 else sin_signed
        fh = jnp.concatenate([first_half] * reps, axis=-1) if reps > 1 else first_half
        nxt = pltpu.roll(t, width - HEAD_DIM // 2, axis=1)
        prv = pltpu.roll(t, HEAD_DIM // 2, axis=1)
        return t * c + jnp.where(fh, nxt, prv) * s

    o = 0
    q_ref[...] = (rope(proj(o, ATTN_WIDTH)) * (HEAD_DIM ** -0.5)).astype(BF16)
    o += ATTN_WIDTH
    k_ref[...] = rope(proj(o, KV_WIDTH)).astype(BF16)
    o += KV_WIDTH
    v_ref[...] = proj(o, KV_WIDTH).astype(BF16)
    o += KV_WIDTH
    xr_ref[...] = proj(o, LRU_WIDTH).astype(BF16)
    o += LRU_WIDTH
    gy_ref[...] = jax.nn.gelu(proj(o, LRU_WIDTH)).astype(BF16)
    o += LRU_WIDTH
    sga_ref[...] = _sigmoid(proj(o, D_MODEL)).astype(BF16)
    o += D_MODEL
    sgl_ref[...] = _sigmoid(proj(o, D_MODEL)).astype(BF16)


def _in_proj(x2, pos2, g, invf, w_in_bf, tm):
    n = x2.shape[0]
    row = lambda w: pl.BlockSpec((tm, w), lambda i: (i, 0))
    full = lambda a: pl.BlockSpec(a.shape, lambda i: (0,) * a.ndim)
    widths = (ATTN_WIDTH, KV_WIDTH, KV_WIDTH, LRU_WIDTH, LRU_WIDTH, D_MODEL, D_MODEL)
    return pl.pallas_call(
        _in_proj_kernel,
        grid=(n // tm,),
        in_specs=[row(D_MODEL), row(1), full(g), full(invf), full(w_in_bf)],
        out_specs=[row(w) for w in widths],
        out_shape=[jax.ShapeDtypeStruct((n, w), BF16) for w in widths],
        compiler_params=pltpu.CompilerParams(
            dimension_semantics=("parallel",), vmem_limit_bytes=VMEM_LIMIT),
        name="in_proj",
    )(x2, pos2, g, invf, w_in_bf)


def _attn_kernel(sink_ref, q_ref, kc_ref, kp_ref, vc_ref, vp_ref, o_ref):
    nblk = q_ref.shape[0] // BLOCK
    j = pl.program_id(1)
    row = lax.broadcasted_iota(jnp.int32, (BLOCK, 2 * BLOCK), 0)
    col = lax.broadcasted_iota(jnp.int32, (BLOCK, 2 * BLOCK), 1)
    band = (col > row) & (col <= row + BLOCK)
    for i in range(nblk):
        rows = slice(i * BLOCK, (i + 1) * BLOCK)
        q = q_ref[rows, :]
        if i == 0:
            k_prev, v_prev = kp_ref[...], vp_ref[...]
            mask = band & ((col >= BLOCK) | (j > 0))
        else:
            before = slice((i - 1) * BLOCK, i * BLOCK)
            k_prev, v_prev = kc_ref[before, :], vc_ref[before, :]
            mask = band
        kk = jnp.concatenate([k_prev, kc_ref[rows, :]], axis=0)
        vv = jnp.concatenate([v_prev, vc_ref[rows, :]], axis=0)
        outs = []
        for h in range(N_Q_HEADS):
            g = h // GROUP
            qh = q[:, h * HEAD_DIM:(h + 1) * HEAD_DIM]
            kg = kk[:, g * HEAD_DIM:(g + 1) * HEAD_DIM]
            vg = vv[:, g * HEAD_DIM:(g + 1) * HEAD_DIM]
            s = lax.dot_general(qh, kg, (((1,), (1,)), ((), ())), preferred_element_type=F32)
            s = jnp.where(mask, s, NEG_INF)
            sink = sink_ref[h]
            m = jnp.maximum(jnp.max(s, axis=-1, keepdims=True), sink)
            p = jnp.exp(s - m)
            denom = jnp.sum(p, axis=-1, keepdims=True) + jnp.exp(sink - m)
            oh = jnp.dot(p.astype(BF16), vg, preferred_element_type=F32)
            outs.append(oh / denom)
        o_ref[rows, :] = jnp.concatenate(outs, axis=-1).astype(BF16)


def _attention(sinks, q, k, v, bsz, seq, tq):
    n = q.shape[0]
    nt = seq // tq
    per = tq // BLOCK
    cur = lambda w: pl.BlockSpec((tq, w), lambda b, j: (b * nt + j, 0))
    prev = lambda w: pl.BlockSpec(
        (BLOCK, w), lambda b, j: (b * nt * per + jnp.maximum(j * per - 1, 0), 0))
    return pl.pallas_call(
        _attn_kernel,
        grid=(bsz, nt),
        in_specs=[pl.BlockSpec(memory_space=pltpu.SMEM),
                  cur(ATTN_WIDTH), cur(KV_WIDTH), prev(KV_WIDTH), cur(KV_WIDTH), prev(KV_WIDTH)],
        out_specs=cur(ATTN_WIDTH),
        out_shape=jax.ShapeDtypeStruct((n, ATTN_WIDTH), BF16),
        compiler_params=pltpu.CompilerParams(
            dimension_semantics=("parallel", "parallel"), vmem_limit_bytes=VMEM_LIMIT),
        name="attn",
    )(sinks, q, k, k, v, v)


def _lru_kernel(xr_ref, gy_ref, cw_ref, cb_ref, wg_ref, bg_ref, lam_ref, o_ref,
                tail_ref, h_ref):
    ts = xr_ref.shape[0]

    @pl.when(pl.program_id(1) == 0)
    def _():
        tail_ref[...] = jnp.zeros_like(tail_ref)
        h_ref[...] = jnp.zeros_like(h_ref)

    x = xr_ref[...].astype(F32)
    prev = tail_ref[...]
    row8 = lax.broadcasted_iota(jnp.int32, (8, LRU_WIDTH), 0)
    cw = cw_ref[...]
    xc = x * cw[CONV_WIDTH - 1:CONV_WIDTH] + cb_ref[...]
    for d in range(1, CONV_WIDTH):
        xs = pltpu.roll(x, d, axis=0)
        ps = pltpu.roll(prev, d, axis=0)
        head = jnp.where(row8 < d, ps, xs[0:8])
        xs = jnp.concatenate([head, xs[8:]], axis=0)
        xc = xc + xs * cw[CONV_WIDTH - 1 - d:CONV_WIDTH - d]
    tail_ref[...] = x[ts - 8:ts]

    gates = jnp.dot(xc.astype(BF16), wg_ref[...], preferred_element_type=F32) + bg_ref[...]
    r = _sigmoid(gates[:, :LRU_WIDTH])
    i = _sigmoid(gates[:, LRU_WIDTH:])
    log_a = (-LRU_C) * r * jax.nn.softplus(-lam_ref[...])
    a = jnp.exp(log_a)
    u = jnp.sqrt(1.0 - a * a) * (i * xc)

    groups = ts // SUBLANES
    a = a.reshape(groups, SUBLANES, LRU_WIDTH)
    u = u.reshape(groups, SUBLANES, LRU_WIDTH)
    sub = lax.broadcasted_iota(jnp.int32, (groups, SUBLANES, LRU_WIDTH), 1)
    d = 1
    while d < SUBLANES:
        keep = sub >= d
        a_s = jnp.where(keep, pltpu.roll(a, d, axis=1), 1.0)
        u_s = jnp.where(keep, pltpu.roll(u, d, axis=1), 0.0)
        u = a * u_s + u
        a = a * a_s
        d *= 2
    carry = h_ref[0:1]
    hs = []
    for g in range(groups):
        hg = a[g] * carry + u[g]
        hs.append(hg)
        carry = hg[SUBLANES - 1:SUBLANES]
    h_ref[...] = jnp.broadcast_to(carry, h_ref.shape)
    h = jnp.concatenate(hs, axis=0)
    o_ref[...] = (h * gy_ref[...].astype(F32)).astype(BF16)


def _lru(xr, gy, cw, cb, wg, bg, lam, bsz, seq, ts):
    n = xr.shape[0]
    nt = seq // ts
    tile = pl.BlockSpec((ts, LRU_WIDTH), lambda b, j: (b * nt + j, 0))
    full = lambda a: pl.BlockSpec(a.shape, lambda b, j: (0,) * a.ndim)
    return pl.pallas_call(
        _lru_kernel,
        grid=(bsz, nt),
        in_specs=[tile, tile, full(cw), full(cb), full(wg), full(bg), full(lam)],
        out_specs=tile,
        out_shape=jax.ShapeDtypeStruct((n, LRU_WIDTH), BF16),
        scratch_shapes=[pltpu.VMEM((8, LRU_WIDTH), F32), pltpu.VMEM((8, LRU_WIDTH), F32)],
        compiler_params=pltpu.CompilerParams(
            dimension_semantics=("parallel", "arbitrary"), vmem_limit_bytes=VMEM_LIMIT),
        name="lru",
    )(xr, gy, cw, cb, wg, bg, lam)


def _merge_kernel(o_ref, hl_ref, sga_ref, sgl_ref, x_ref, wao_ref, wlo_ref, wout_ref,
                  g_ref, wrh_ref, wrl_ref, br_ref,
                  x1_ref, hc_ref, meta_ref, cnt_ref):
    last = pl.num_programs(0) - 1

    @pl.when(pl.program_id(0) == last)
    def _():
        hc_ref[...] = jnp.zeros_like(hc_ref)

    @pl.when(pl.program_id(0) < last)
    def _():
        _merge_tile(o_ref, hl_ref, sga_ref, sgl_ref, x_ref, wao_ref, wlo_ref, wout_ref,
                    g_ref, wrh_ref, wrl_ref, br_ref, x1_ref, hc_ref, meta_ref, cnt_ref)


def _merge_tile(o_ref, hl_ref, sga_ref, sgl_ref, x_ref, wao_ref, wlo_ref, wout_ref,
                g_ref, wrh_ref, wrl_ref, br_ref,
                x1_ref, hc_ref, meta_ref, cnt_ref):
    tm = x_ref.shape[0]
    cap = hc_ref.shape[0]

    ya = jnp.dot(o_ref[...], wao_ref[...], preferred_element_type=F32)
    yl = jnp.dot(hl_ref[...], wlo_ref[...], preferred_element_type=F32)
    merged = sga_ref[...].astype(F32) * ya + sgl_ref[...].astype(F32) * yl
    x1 = x_ref[...] + jnp.dot(merged.astype(BF16), wout_ref[...], preferred_element_type=F32)
    x1_ref[...] = x1
    h2 = _rms(x1, g_ref[...])
    h2b = h2.astype(BF16)

    h2l = (h2 - h2b.astype(F32)).astype(BF16)
    logits = (jnp.dot(h2b, wrh_ref[...], preferred_element_type=F32)
              + jnp.dot(h2l, wrh_ref[...], preferred_element_type=F32)
              + jnp.dot(h2b, wrl_ref[...], preferred_element_type=F32)
              + br_ref[...])
    lt = jnp.transpose(logits)[:N_EXPERTS]
    e_id = lax.broadcasted_iota(jnp.int32, (N_EXPERTS, tm), 0)
    work = lt
    vals, sels = [], []
    for _ in range(TOP_K):
        m = jnp.max(work, axis=0, keepdims=True)
        sel = jnp.min(jnp.where(work == m, e_id, N_EXPERTS), axis=0, keepdims=True)
        vals.append(m)
        sels.append(sel)
        work = jnp.where(e_id == sel, -jnp.inf, work)
    exps = [jnp.exp(v - vals[0]) for v in vals]
    tot = exps[0] + exps[1] + exps[2] + exps[3]

    onehot = jnp.zeros((N_EXPERTS, tm), F32)
    for sel in sels:
        onehot = onehot + (e_id == sel).astype(F32)
    s_i = lax.broadcasted_iota(jnp.int32, (tm, tm), 0)
    t_i = lax.broadcasted_iota(jnp.int32, (tm, tm), 1)
    earlier = jnp.dot(onehot.astype(BF16), (s_i < t_i).astype(BF16),
                      preferred_element_type=F32)
    cnt = jnp.sum(onehot, axis=1, keepdims=True)
    chunks = jnp.floor((cnt + (CHUNK - 1)) * (1.0 / CHUNK))
    e_r = lax.broadcasted_iota(jnp.int32, (N_EXPERTS, N_EXPERTS), 0)
    e_c = lax.broadcasted_iota(jnp.int32, (N_EXPERTS, N_EXPERTS), 1)
    start = jnp.dot((e_c < e_r).astype(BF16),
                    jnp.broadcast_to(chunks, (N_EXPERTS, LANES)).astype(BF16),
                    preferred_element_type=F32)[:, 0:1] * float(CHUNK)
    slot_of = earlier + start

    pos_rows = [jnp.sum(jnp.where(e_id == sels[k], slot_of, 0.0), axis=0, keepdims=True)
                for k in range(TOP_K)]
    r_id = lax.broadcasted_iota(jnp.int32, (2 * TOP_K, tm), 0)
    meta = jnp.zeros((2 * TOP_K, tm), F32)
    for k in range(TOP_K):
        meta = jnp.where(r_id == k, pos_rows[k], meta)
        meta = jnp.where(r_id == TOP_K + k, exps[k] / tot, meta)
    meta = jnp.concatenate([meta, jnp.zeros((LANES - 2 * TOP_K, tm), F32)], axis=0)
    meta_ref[...] = jnp.transpose(meta)
    cnt_ref[...] = jnp.broadcast_to(cnt, cnt_ref.shape)

    pos_t = [p.astype(jnp.int32).astype(jnp.int16) for p in pos_rows]
    rows_per = cap // PERM_SPLIT
    for c in range(PERM_SPLIT):
        slot_id = lax.broadcasted_iota(jnp.int16, (rows_per, tm), 0) + c * rows_per
        perm = jnp.zeros((rows_per, tm), BF16)
        for k in range(TOP_K):
            perm = jnp.where(slot_id == pos_t[k], 1.0, perm).astype(BF16)
        rows = jnp.dot(perm, h2b, preferred_element_type=F32)
        hc_ref[c * rows_per:(c + 1) * rows_per, :] = _pack_row(rows)


def _merge(o, hl, sga, sgl, x2, wao, wlo, wout, g, wrh, wrl, br, tm, cap):
    n = x2.shape[0]
    ntt = n // tm
    row = lambda w: pl.BlockSpec((tm, w), lambda i: (jnp.minimum(i, ntt - 1), 0))
    full = lambda a: pl.BlockSpec(a.shape, lambda i: (0,) * a.ndim)
    return pl.pallas_call(
        _merge_kernel,
        grid=(ntt + 1,),
        in_specs=[row(ATTN_WIDTH), row(LRU_WIDTH), row(D_MODEL), row(D_MODEL), row(D_MODEL),
                  full(wao), full(wlo), full(wout), full(g), full(wrh), full(wrl), full(br)],
        out_specs=[row(D_MODEL), pl.BlockSpec((cap, HALF), lambda i: (i, 0)),
                   row(LANES),
                   pl.BlockSpec((N_EXPERTS, LANES), lambda i: (jnp.minimum(i, ntt - 1), 0))],
        out_shape=[jax.ShapeDtypeStruct((n, D_MODEL), F32),
                   jax.ShapeDtypeStruct(((ntt + 1) * cap, HALF), jnp.uint32),
                   jax.ShapeDtypeStruct((n, LANES), F32),
                   jax.ShapeDtypeStruct((ntt * N_EXPERTS, LANES), F32)],
        compiler_params=pltpu.CompilerParams(
            dimension_semantics=("arbitrary",), vmem_limit_bytes=VMEM_LIMIT),
        name="merge_router",
    )(o, hl, sga, sgl, x2, wao, wlo, wout, g, wrh, wrl, br)


def _expert_kernel(te_ref, na_ref, src_ref, dst_ref, hc_hbm_ref, wgu_ref, bgu_ref, wd_ref, bd_ref,
                   yc_ref, xb0, xb1, yb0, yb1, wgu_bf, wd_bf, sem_in, sem_out):
    del hc_hbm_ref
    i = pl.program_id(0)
    na = na_ref[0]
    xb = (xb0, xb1)
    yb = (yb0, yb1)
    n_chunks = xb0.shape[0] // CHUNK

    def hbm_rows(chunk):
        return yc_ref.at[pl.ds(pl.multiple_of(chunk * CHUNK, CHUNK), CHUNK)]

    def start_in(tile, s):
        for j in range(n_chunks):
            pltpu.make_async_copy(hbm_rows(src_ref[tile * n_chunks + j]),
                                  xb[s].at[pl.ds(j * CHUNK, CHUNK)], sem_in.at[s]).start()

    def start_out(tile, s):
        for j in range(n_chunks):
            pltpu.make_async_copy(yb[s].at[pl.ds(j * CHUNK, CHUNK)],
                                  hbm_rows(dst_ref[tile * n_chunks + j]), sem_out.at[s]).start()

    def wait_in(s):
        pltpu.make_async_copy(xb[s], xb[s], sem_in.at[s]).wait()

    def wait_out(s):
        pltpu.make_async_copy(yb[s], yb[s], sem_out.at[s]).wait()

    @pl.when(i == 0)
    def _():
        start_in(0, 0)

    @pl.when((i < na) & ((i == 0) | (te_ref[i] != te_ref[jnp.maximum(i - 1, 0)])))
    def _():
        rows = 128

        def cast_gu(r, c):
            sl = pl.ds(pl.multiple_of(r * rows, rows), rows)
            wgu_bf[sl, :] = wgu_ref[0, sl, :].astype(BF16)
            return c

        def cast_d(r, c):
            sl = pl.ds(pl.multiple_of(r * rows, rows), rows)
            wd_bf[sl, :] = wd_ref[0, sl, :].astype(BF16)
            return c

        lax.fori_loop(0, D_MODEL // rows, cast_gu, 0)
        lax.fori_loop(0, D_EXPERT // rows, cast_d, 0)

    def tile_step(s):
        @pl.when(i >= 2)
        def _():
            wait_out(s)

        @pl.when(i + 1 < na)
        def _():
            start_in(i + 1, 1 - s)

        wait_in(s)
        w = xb[s][...]
        lo = _unpack_row(w, 0).astype(BF16)
        hi = _unpack_row(w, 1).astype(BF16)
        gu = (jnp.dot(lo, wgu_bf[:HALF, :], preferred_element_type=F32)
              + jnp.dot(hi, wgu_bf[HALF:, :], preferred_element_type=F32)
              + bgu_ref[0])
        g = jnp.minimum(gu[:, :D_EXPERT], SWIGLU_LIMIT)
        u = jnp.clip(gu[:, D_EXPERT:], -SWIGLU_LIMIT, SWIGLU_LIMIT)
        act = (u + 1.0) * (g * jax.nn.sigmoid(SWIGLU_ALPHA * g))
        y = jnp.dot(act.astype(BF16), wd_bf[...], preferred_element_type=F32) + bd_ref[0]
        yb[s][...] = _pack_row(y)
        start_out(i, s)

        @pl.when(i == na - 1)
        def _():
            @pl.when(i >= 1)
            def _():
                wait_out(1 - s)

            wait_out(s)

    for s in range(2):
        pl.when((i < na) & (i % 2 == s))(functools.partial(tile_step, s))


def _experts(tile_e, n_active, chunk_src, chunk_dst, hc, wgu, bgu, wd, bd, tile_rows, n_tiles):
    per_e = lambda a: pl.BlockSpec((1,) + a.shape[1:], lambda i, te, na, src, dst: (te[i], 0, 0))
    tile_buf = pltpu.VMEM((tile_rows, HALF), jnp.uint32)
    return pl.pallas_call(
        _expert_kernel,
        grid_spec=pltpu.PrefetchScalarGridSpec(
            num_scalar_prefetch=4,
            grid=(n_tiles,),
            in_specs=[pl.BlockSpec(memory_space=pl.ANY),
                      per_e(wgu), per_e(bgu), per_e(wd), per_e(bd)],
            out_specs=pl.BlockSpec(memory_space=pl.ANY),
            scratch_shapes=[tile_buf, tile_buf, tile_buf, tile_buf,
                            pltpu.VMEM(wgu.shape[1:], BF16), pltpu.VMEM(wd.shape[1:], BF16),
                            pltpu.SemaphoreType.DMA((2,)),
                            pltpu.SemaphoreType.DMA((2,))],
        ),
        out_shape=jax.ShapeDtypeStruct(hc.shape, jnp.uint32),
        input_output_aliases={4: 0},
        compiler_params=pltpu.CompilerParams(
            dimension_semantics=("arbitrary",), vmem_limit_bytes=VMEM_LIMIT),
        name="experts",
    )(tile_e, n_active, chunk_src, chunk_dst, hc, wgu, bgu, wd, bd)


def _combine_kernel(yc_ref, meta_ref, x1_ref, g_ref, o_ref):
    tm = x1_ref.shape[0]
    cap = yc_ref.shape[0]
    meta = meta_ref[...]
    pos = meta[:, :TOP_K].astype(jnp.int32).astype(jnp.int16)
    gate = meta[:, TOP_K:2 * TOP_K].astype(BF16)
    slot_id = lax.broadcasted_iota(jnp.int16, (tm, cap), 1)
    sel = jnp.zeros((tm, cap), BF16)
    for k in range(TOP_K):
        sel = jnp.where(slot_id == pos[:, k:k + 1], gate[:, k:k + 1], sel)
    w = yc_ref[...]
    lo = jnp.dot(sel, _unpack_row(w, 0).astype(BF16), preferred_element_type=F32)
    hi = jnp.dot(sel, _unpack_row(w, 1).astype(BF16), preferred_element_type=F32)
    x2 = x1_ref[...] + jnp.concatenate([lo, hi], axis=-1)
    o_ref[...] = _rms(x2, g_ref[...])


def _combine(yc, meta, x1, g, tm, cap):
    n = x1.shape[0]
    row = lambda w: pl.BlockSpec((tm, w), lambda i: (i, 0))
    return pl.pallas_call(
        _combine_kernel,
        grid=(n // tm,),
        in_specs=[pl.BlockSpec((cap, HALF), lambda i: (i, 0)), row(LANES),
                  row(D_MODEL), pl.BlockSpec((1, D_MODEL), lambda i: (0, 0))],
        out_specs=row(D_MODEL),
        out_shape=jax.ShapeDtypeStruct((n, D_MODEL), F32),
        compiler_params=pltpu.CompilerParams(
            dimension_semantics=("parallel",), vmem_limit_bytes=VMEM_LIMIT),
        name="combine",
    )(yc, meta, x1, g)


def _block_diag(w):
    eye = jnp.eye(LRU_HEADS, dtype=w.dtype)
    return jnp.einsum("hij,hg->higj", w, eye).reshape(LRU_WIDTH, LRU_WIDTH)


def _pick(n, pref):
    t = pref
    while n % t:
        t //= 2
    return t


def _tile_plan(cnt, tm, cap, tile_rows, n_tiles):
    i32 = jnp.int32
    ntt = cnt.shape[0]
    cpt = tile_rows // CHUNK
    pcc = (cnt + CHUNK - 1) // CHUNK
    seg_base = jnp.arange(ntt, dtype=i32)[:, None] * (cap // CHUNK) + jnp.cumsum(pcc, axis=1) - pcc
    seg_end = jnp.cumsum(pcc, axis=0)
    seg_start = seg_end - pcc
    tot = seg_end[-1]
    ntile = (tot + cpt - 1) // cpt
    tile_end = jnp.cumsum(ntile)
    tile_start = tile_end - ntile
    n_active = tile_end[-1]
    ids = jnp.arange(n_tiles + 1, dtype=i32)
    live = ids < n_active
    tid = jnp.minimum(ids, n_active - 1)
    tile_e = jnp.sum((tile_end[None, :] <= tid[:, None]).astype(i32), axis=1)
    q = ((tid - tile_start[tile_e]) * cpt)[:, None] + jnp.arange(cpt, dtype=i32)[None, :]
    starts = seg_start.T[tile_e][:, None, :]
    ends = seg_end.T[tile_e][:, None, :]
    shift = (seg_base - seg_start).T[tile_e][:, None, :]
    inside = (starts <= q[:, :, None]) & (q[:, :, None] < ends)
    chunk = q + jnp.sum(jnp.where(inside, shift, 0), axis=2)
    valid = live[:, None] & (q < tot[tile_e][:, None])
    spare = ntt * (cap // CHUNK)
    slot_spare = spare + (ids % 2)[:, None] * cpt + jnp.arange(cpt, dtype=i32)[None, :]
    src = jnp.where(valid, chunk, spare + 2 * cpt)
    dst = jnp.where(valid, chunk, slot_spare)
    return (tile_e[:n_tiles].astype(i32), n_active.reshape(1).astype(i32),
            src.reshape(-1).astype(i32), dst.reshape(-1).astype(i32))


def _layer(x2, pos2, bsz, seq, norm_mix_g, w_in, conv_w, conv_b, w_rg, b_rg, w_ig, b_ig,
           lru_lambda, attn_sinks, w_attn_o, w_lru_o, w_out, norm_ffn_g, w_router, b_router,
           w_gate_up, b_gate_up, w_down, b_down, norm_out_g):
    n = x2.shape[0]
    tm = _pick(n, TOKEN_TILE)
    row = lambda a: a.reshape(1, -1).astype(F32)

    half = np.arange(0, HEAD_DIM, 2, dtype=np.float32) / HEAD_DIM
    inv_freq = (1.0 / (ROPE_THETA ** jnp.asarray(half, F32)))
    invf = jnp.tile(inv_freq, LANES // (HEAD_DIM // 2)).reshape(1, LANES)

    q, k, v, xr, gy, sga, sgl = _in_proj(x2, pos2, row(norm_mix_g), invf, w_in.astype(BF16), tm)
    o = _attention(attn_sinks.astype(F32), q, k, v, bsz, seq, _pick(seq, ATTN_TILE))

    wg = jnp.concatenate([_block_diag(w_rg), _block_diag(w_ig)], axis=1).astype(BF16)
    bg = jnp.concatenate([b_rg, b_ig]).reshape(1, -1).astype(F32)
    hl = _lru(xr, gy, conv_w.astype(F32), row(conv_b), wg, bg, row(lru_lambda),
              bsz, seq, _pick(seq, LRU_TILE))

    cap = tm * TOP_K + N_EXPERTS * CHUNK
    wr = jnp.zeros((D_MODEL, LANES), F32).at[:, :N_EXPERTS].set(w_router.astype(F32))
    wrh = wr.astype(BF16)
    wrl = (wr - wrh.astype(F32)).astype(BF16)
    br = jnp.full((1, LANES), NEG_INF, F32).at[0, :N_EXPERTS].set(b_router.astype(F32))
    x1, hc, meta, cnt = _merge(
        o, hl, sga, sgl, x2, w_attn_o.astype(BF16), w_lru_o.astype(BF16), w_out.astype(BF16),
        row(norm_ffn_g), wrh, wrl, br, tm, cap)

    ntt = n // tm
    cpt = EXPERT_TILE // CHUNK
    max_chunks = ntt * (tm * TOP_K // CHUNK + N_EXPERTS)
    n_tiles = -(-max_chunks // cpt) + N_EXPERTS
    counts = cnt.reshape(ntt, N_EXPERTS, LANES)[:, :, 0].astype(jnp.int32)
    tile_e, n_active, chunk_src, chunk_dst = _tile_plan(counts, tm, cap, EXPERT_TILE, n_tiles)
    yc = _experts(tile_e, n_active, chunk_src, chunk_dst, hc,
                  w_gate_up.astype(F32), b_gate_up.reshape(N_EXPERTS, 1, -1).astype(F32),
                  w_down.astype(F32), b_down.reshape(N_EXPERTS, 1, -1).astype(F32),
                  EXPERT_TILE, n_tiles)
    return _combine(yc, meta, x1, row(norm_out_g), tm, cap)


def kernel(x, positions, norm_mix_g, w_in, conv_w, conv_b, w_rg, b_rg, w_ig, b_ig, lru_lambda,
           attn_sinks, w_attn_o, w_lru_o, w_out, norm_ffn_g, w_router, b_router, w_gate_up,
           b_gate_up, w_down, b_down, norm_final_g):
    bsz, seq, d = x.shape
    depth = w_in.shape[0]
    assert depth == 1 and d == D_MODEL and seq % BLOCK == 0
    x2 = x.reshape(bsz * seq, d)
    pos2 = positions.reshape(bsz * seq, 1).astype(jnp.int32)
    out = _layer(x2, pos2, bsz, seq, norm_mix_g[0], w_in[0], conv_w[0], conv_b[0], w_rg[0],
                 b_rg[0], w_ig[0], b_ig[0], lru_lambda[0], attn_sinks[0], w_attn_o[0],
                 w_lru_o[0], w_out[0], norm_ffn_g[0], w_router[0], b_router[0], w_gate_up[0],
                 b_gate_up[0], w_down[0], b_down[0], norm_final_g)
    return out.reshape(bsz, seq, d)
```

```python
import functools

import numpy as np
import jax
import jax.numpy as jnp
from jax import lax
from jax.experimental import pallas as pl
from jax.experimental.pallas import tpu as pltpu

F32 = jnp.float32
BF16 = jnp.bfloat16

D_MODEL = 1024
HEAD_DIM = 64
N_Q_HEADS = 8
N_KV_HEADS = 2
GROUP = N_Q_HEADS // N_KV_HEADS
ATTN_WIDTH = N_Q_HEADS * HEAD_DIM
KV_WIDTH = N_KV_HEADS * HEAD_DIM
WINDOW = 128
BLOCK = 128
ROPE_THETA = 10000.0
LRU_WIDTH = D_MODEL // 2
LRU_HEADS = 8
LRU_BLOCK = LRU_WIDTH // LRU_HEADS
LRU_C = 8.0
CONV_WIDTH = 4
N_EXPERTS = 32
TOP_K = 4
D_EXPERT = D_MODEL
SWIGLU_LIMIT = 7.0
SWIGLU_ALPHA = 1.702
NORM_EPS = 1e-6
NEG_INF = -1e30
IN_SPLITS = (ATTN_WIDTH, KV_WIDTH, KV_WIDTH, LRU_WIDTH, LRU_WIDTH, D_MODEL, D_MODEL)
IN_WIDTH = sum(IN_SPLITS)

LANES = 128
SUBLANES = 8
HALF = D_MODEL // 2
VMEM_LIMIT = 56 * 1024 * 1024

TOKEN_TILE = 512
ATTN_TILE = 512
LRU_TILE = 512
EXPERT_TILE = 512
CHUNK = 8
PERM_SPLIT = 4


def _rms(x, g):
    return x * lax.rsqrt(jnp.mean(x * x, axis=-1, keepdims=True) + NORM_EPS) * g


def _sigmoid(z):
    return 0.5 * jnp.tanh(0.5 * z) + 0.5


def _pack_row(v):
    return pltpu.pack_elementwise([v[:, :HALF], v[:, HALF:]], packed_dtype=BF16)


def _unpack_row(w, index):
    return pltpu.unpack_elementwise(w, index=index, packed_dtype=BF16, unpacked_dtype=F32)


def _in_proj_kernel(x_ref, pos_ref, g_ref, invf_ref, w_ref,
                    q_ref, k_ref, v_ref, xr_ref, gy_ref, sga_ref, sgl_ref):
    h = _rms(x_ref[...], g_ref[...]).astype(BF16)

    def proj(lo, width):
        return jnp.dot(h, w_ref[:, lo:lo + width], preferred_element_type=F32)

    ang = pos_ref[...].astype(F32) * invf_ref[...]
    cos = jnp.cos(ang)
    sin = jnp.sin(ang)
    lane = lax.broadcasted_iota(jnp.int32, (1, LANES), 1)
    first_half = (lane % HEAD_DIM) < (HEAD_DIM // 2)
    sin_signed = jnp.where(first_half, -sin, sin)

    def rope(t):
        width = t.shape[-1]
        reps = width // LANES
        c = jnp.concatenate([cos] * reps, axis=-1) if reps > 1 else cos
        s = jnp.concatenate([sin_signed] * reps, axis=-1) if reps > 1 else sin_signed
        fh = jnp.concatenate([first_half] * reps, axis=-1) if reps > 1 else first_half
        nxt = pltpu.roll(t, width - HEAD_DIM // 2, axis=1)
        prv = pltpu.roll(t, HEAD_DIM // 2, axis=1)
        return t * c + jnp.where(fh, nxt, prv) * s

    o = 0
    q_ref[...] = (rope(proj(o, ATTN_WIDTH)) * (HEAD_DIM ** -0.5)).astype(BF16)
    o += ATTN_WIDTH
    k_ref[...] = rope(proj(o, KV_WIDTH)).astype(BF16)
    o += KV_WIDTH
    v_ref[...] = proj(o, KV_WIDTH).astype(BF16)
    o += KV_WIDTH
    xr_ref[...] = proj(o, LRU_WIDTH).astype(BF16)
    o += LRU_WIDTH
    gy_ref[...] = jax.nn.gelu(proj(o, LRU_WIDTH)).astype(BF16)
    o += LRU_WIDTH
    sga_ref[...] = _sigmoid(proj(o, D_MODEL)).astype(BF16)
    o += D_MODEL
    sgl_ref[...] = _sigmoid(proj(o, D_MODEL)).astype(BF16)


def _in_proj(x2, pos2, g, invf, w_in_bf, tm):
    n = x2.shape[0]
    row = lambda w: pl.BlockSpec((tm, w), lambda i: (i, 0))
    full = lambda a: pl.BlockSpec(a.shape, lambda i: (0,) * a.ndim)
    widths = (ATTN_WIDTH, KV_WIDTH, KV_WIDTH, LRU_WIDTH, LRU_WIDTH, D_MODEL, D_MODEL)
    return pl.pallas_call(
        _in_proj_kernel,
        grid=(n // tm,),
        in_specs=[row(D_MODEL), row(1), full(g), full(invf), full(w_in_bf)],
        out_specs=[row(w) for w in widths],
        out_shape=[jax.ShapeDtypeStruct((n, w), BF16) for w in widths],
        compiler_params=pltpu.CompilerParams(
            dimension_semantics=("parallel",), vmem_limit_bytes=VMEM_LIMIT),
        name="in_proj",
    )(x2, pos2, g, invf, w_in_bf)


def _attn_kernel(sink_ref, q_ref, kc_ref, kp_ref, vc_ref, vp_ref, o_ref):
    nblk = q_ref.shape[0] // BLOCK
    j = pl.program_id(1)
    row = lax.broadcasted_iota(jnp.int32, (BLOCK, BLOCK), 0)
    col = lax.broadcasted_iota(jnp.int32, (BLOCK, BLOCK), 1)
    own = col <= row
    ones = jnp.ones((BLOCK, HEAD_DIM), BF16)
    dims = (((1,), (1,)), ((), ()))
    for i in range(nblk):
        rows = slice(i * BLOCK, (i + 1) * BLOCK)
        q = q_ref[rows, :]
        if i == 0:
            k_prev, v_prev = kp_ref[...], vp_ref[...]
            no_prev = jnp.where(j > 0, 0.0, NEG_INF)
        else:
            before = slice((i - 1) * BLOCK, i * BLOCK)
            k_prev, v_prev = kc_ref[before, :], vc_ref[before, :]
            no_prev = None
        k_own, v_own = kc_ref[rows, :], vc_ref[rows, :]
        heads = [slice((h // GROUP) * HEAD_DIM, (h // GROUP + 1) * HEAD_DIM)
                 for h in range(N_Q_HEADS)]
        scores = []
        for h in range(N_Q_HEADS):
            qh = q[:, h * HEAD_DIM:(h + 1) * HEAD_DIM]
            s_own = lax.dot_general(qh, k_own[:, heads[h]], dims, preferred_element_type=F32)
            s_prev = lax.dot_general(qh, k_prev[:, heads[h]], dims, preferred_element_type=F32)
            if no_prev is not None:
                s_prev = s_prev + no_prev
            scores.append(jnp.where(own, s_own, s_prev))
        probs, tails = [], []
        for h in range(N_Q_HEADS):
            sink = sink_ref[h]
            m = jnp.maximum(jnp.max(scores[h], axis=-1, keepdims=True), sink)
            p = jnp.exp(scores[h] - m)
            probs.append((jnp.where(own, p, 0.0).astype(BF16), jnp.where(own, 0.0, p).astype(BF16)))
            tails.append(jnp.exp(sink - m))
        v_ext = [(jnp.concatenate([v_own[:, heads[g * GROUP]], ones], axis=-1),
                  jnp.concatenate([v_prev[:, heads[g * GROUP]], ones], axis=-1))
                 for g in range(N_KV_HEADS)]
        outs = []
        for h in range(N_Q_HEADS):
            vo, vp = v_ext[h // GROUP]
            acc = (jnp.dot(probs[h][0], vo, preferred_element_type=F32)
                   + jnp.dot(probs[h][1], vp, preferred_element_type=F32))
            denom = acc[:, HEAD_DIM:HEAD_DIM + 1] + tails[h]
            outs.append(acc[:, :HEAD_DIM] / denom)
        o_ref[rows, :] = jnp.concatenate(outs, axis=-1).astype(BF16)


def _attention(sinks, q, k, v, bsz, seq, tq):
    n = q.shape[0]
    nt = seq // tq
    per = tq // BLOCK
    cur = lambda w: pl.BlockSpec((tq, w), lambda b, j: (b * nt + j, 0))
    prev = lambda w: pl.BlockSpec(
        (BLOCK, w), lambda b, j: (b * nt * per + jnp.maximum(j * per - 1, 0), 0))
    return pl.pallas_call(
        _attn_kernel,
        grid=(bsz, nt),
        in_specs=[pl.BlockSpec(memory_space=pltpu.SMEM),
                  cur(ATTN_WIDTH), cur(KV_WIDTH), prev(KV_WIDTH), cur(KV_WIDTH), prev(KV_WIDTH)],
        out_specs=cur(ATTN_WIDTH),
        out_shape=jax.ShapeDtypeStruct((n, ATTN_WIDTH), BF16),
        compiler_params=pltpu.CompilerParams(
            dimension_semantics=("parallel", "parallel"), vmem_limit_bytes=VMEM_LIMIT),
        name="attn",
    )(sinks, q, k, k, v, v)


def _lru_kernel(xr_ref, gy_ref, cw_ref, cb_ref, wg_ref, bg_ref, lam_ref, o_ref,
                tail_ref, h_ref):
    ts = xr_ref.shape[0]

    @pl.when(pl.program_id(1) == 0)
    def _():
        tail_ref[...] = jnp.zeros_like(tail_ref)
        h_ref[...] = jnp.zeros_like(h_ref)

    x = xr_ref[...].astype(F32)
    prev = tail_ref[...]
    row8 = lax.broadcasted_iota(jnp.int32, (8, LRU_WIDTH), 0)
    cw = cw_ref[...]
    xc = x * cw[CONV_WIDTH - 1:CONV_WIDTH] + cb_ref[...]
    for d in range(1, CONV_WIDTH):
        xs = pltpu.roll(x, d, axis=0)
        ps = pltpu.roll(prev, d, axis=0)
        head = jnp.where(row8 < d, ps, xs[0:8])
        xs = jnp.concatenate([head, xs[8:]], axis=0)
        xc = xc + xs * cw[CONV_WIDTH - 1 - d:CONV_WIDTH - d]
    tail_ref[...] = x[ts - 8:ts]

    gates = jnp.dot(xc.astype(BF16), wg_ref[...], preferred_element_type=F32) + bg_ref[...]
    r = _sigmoid(gates[:, :LRU_WIDTH])
    i = _sigmoid(gates[:, LRU_WIDTH:])
    log_a = (-LRU_C) * r * jax.nn.softplus(-lam_ref[...])
    a = jnp.exp(log_a)
    y = 1.0 - a * a
    u = jnp.where(y > 0.0, y * lax.rsqrt(y), 0.0) * (i * xc)

    groups = ts // SUBLANES
    a = a.reshape(groups, SUBLANES, LRU_WIDTH)
    u = u.reshape(groups, SUBLANES, LRU_WIDTH)
    sub = lax.broadcasted_iota(jnp.int32, (groups, SUBLANES, LRU_WIDTH), 1)
    d = 1
    while d < SUBLANES:
        keep = sub >= d
        a_s = jnp.where(keep, pltpu.roll(a, d, axis=1), 1.0)
        u_s = jnp.where(keep, pltpu.roll(u, d, axis=1), 0.0)
        u = a * u_s + u
        a = a * a_s
        d *= 2
    carry = h_ref[0:1]
    hs = []
    for g in range(groups):
        hg = a[g] * carry + u[g]
        hs.append(hg)
        carry = hg[SUBLANES - 1:SUBLANES]
    h_ref[...] = jnp.broadcast_to(carry, h_ref.shape)
    h = jnp.concatenate(hs, axis=0)
    o_ref[...] = (h * gy_ref[...].astype(F32)).astype(BF16)


def _lru(xr, gy, cw, cb, wg, bg, lam, bsz, seq, ts):
    n = xr.shape[0]
    nt = seq // ts
    tile = pl.BlockSpec((ts, LRU_WIDTH), lambda b, j: (b * nt + j, 0))
    full = lambda a: pl.BlockSpec(a.shape, lambda b, j: (0,) * a.ndim)
    return pl.pallas_call(
        _lru_kernel,
        grid=(bsz, nt),
        in_specs=[tile, tile, full(cw), full(cb), full(wg), full(bg), full(lam)],
        out_specs=tile,
        out_shape=jax.ShapeDtypeStruct((n, LRU_WIDTH), BF16),
        scratch_shapes=[pltpu.VMEM((8, LRU_WIDTH), F32), pltpu.VMEM((8, LRU_WIDTH), F32)],
        compiler_params=pltpu.CompilerParams(
            dimension_semantics=("parallel", "arbitrary"), vmem_limit_bytes=VMEM_LIMIT),
        name="lru",
    )(xr, gy, cw, cb, wg, bg, lam)


def _merge_kernel(o_ref, hl_ref, sga_ref, sgl_ref, x_ref, wao_ref, wlo_ref, wout_ref,
                  g_ref, wrh_ref, wrl_ref, br_ref,
                  x1_ref, hc_ref, meta_ref, cnt_ref):
    last = pl.num_programs(0) - 1

    @pl.when(pl.program_id(0) == last)
    def _():
        hc_ref[...] = jnp.zeros_like(hc_ref)

    @pl.when(pl.program_id(0) < last)
    def _():
        _merge_tile(o_ref, hl_ref, sga_ref, sgl_ref, x_ref, wao_ref, wlo_ref, wout_ref,
                    g_ref, wrh_ref, wrl_ref, br_ref, x1_ref, hc_ref, meta_ref, cnt_ref)


def _merge_tile(o_ref, hl_ref, sga_ref, sgl_ref, x_ref, wao_ref, wlo_ref, wout_ref,
                g_ref, wrh_ref, wrl_ref, br_ref,
                x1_ref, hc_ref, meta_ref, cnt_ref):
    tm = x_ref.shape[0]
    cap = hc_ref.shape[0]

    ya = jnp.dot(o_ref[...], wao_ref[...], preferred_element_type=F32)
    yl = jnp.dot(hl_ref[...], wlo_ref[...], preferred_element_type=F32)
    merged = sga_ref[...].astype(F32) * ya + sgl_ref[...].astype(F32) * yl
    x1 = x_ref[...] + jnp.dot(merged.astype(BF16), wout_ref[...], preferred_element_type=F32)
    x1_ref[...] = x1
    h2 = _rms(x1, g_ref[...])
    h2b = h2.astype(BF16)

    h2l = (h2 - h2b.astype(F32)).astype(BF16)
    logits = (jnp.dot(h2b, wrh_ref[...], preferred_element_type=F32)
              + jnp.dot(h2l, wrh_ref[...], preferred_element_type=F32)
              + jnp.dot(h2b, wrl_ref[...], preferred_element_type=F32)
              + br_ref[...])
    lt = jnp.transpose(logits)[:N_EXPERTS]
    e_id = lax.broadcasted_iota(jnp.int32, (N_EXPERTS, tm), 0)
    work = lt
    vals, sels = [], []
    for _ in range(TOP_K):
        m = jnp.max(work, axis=0, keepdims=True)
        sel = jnp.min(jnp.where(work == m, e_id, N_EXPERTS), axis=0, keepdims=True)
        vals.append(m)
        sels.append(sel)
        work = jnp.where(e_id == sel, -jnp.inf, work)
    exps = [jnp.exp(v - vals[0]) for v in vals]
    tot = exps[0] + exps[1] + exps[2] + exps[3]

    onehot = jnp.zeros((N_EXPERTS, tm), F32)
    for sel in sels:
        onehot = onehot + (e_id == sel).astype(F32)
    s_i = lax.broadcasted_iota(jnp.int32, (tm, tm), 0)
    t_i = lax.broadcasted_iota(jnp.int32, (tm, tm), 1)
    earlier = jnp.dot(onehot.astype(BF16), (s_i < t_i).astype(BF16),
                      preferred_element_type=F32)
    cnt = jnp.sum(onehot, axis=1, keepdims=True)
    chunks = jnp.floor((cnt + (CHUNK - 1)) * (1.0 / CHUNK))
    e_r = lax.broadcasted_iota(jnp.int32, (N_EXPERTS, N_EXPERTS), 0)
    e_c = lax.broadcasted_iota(jnp.int32, (N_EXPERTS, N_EXPERTS), 1)
    start = jnp.dot((e_c < e_r).astype(BF16),
                    jnp.broadcast_to(chunks, (N_EXPERTS, LANES)).astype(BF16),
                    preferred_element_type=F32)[:, 0:1] * float(CHUNK)
    slot_of = earlier + start

    pos_rows = [jnp.sum(jnp.where(e_id == sels[k], slot_of, 0.0), axis=0, keepdims=True)
                for k in range(TOP_K)]
    r_id = lax.broadcasted_iota(jnp.int32, (2 * TOP_K, tm), 0)
    meta = jnp.zeros((2 * TOP_K, tm), F32)
    for k in range(TOP_K):
        meta = jnp.where(r_id == k, pos_rows[k], meta)
        meta = jnp.where(r_id == TOP_K + k, exps[k] / tot, meta)
    meta = jnp.concatenate([meta, jnp.zeros((LANES - 2 * TOP_K, tm), F32)], axis=0)
    meta_ref[...] = jnp.transpose(meta)
    cnt_ref[...] = jnp.broadcast_to(cnt, cnt_ref.shape)

    pos_t = [p.astype(jnp.int32).astype(jnp.int16) for p in pos_rows]
    rows_per = cap // PERM_SPLIT
    for c in range(PERM_SPLIT):
        slot_id = lax.broadcasted_iota(jnp.int16, (rows_per, tm), 0) + c * rows_per
        perm = jnp.zeros((rows_per, tm), BF16)
        for k in range(TOP_K):
            perm = jnp.where(slot_id == pos_t[k], 1.0, perm).astype(BF16)
        rows = jnp.dot(perm, h2b, preferred_element_type=F32)
        hc_ref[c * rows_per:(c + 1) * rows_per, :] = _pack_row(rows)


def _merge(o, hl, sga, sgl, x2, wao, wlo, wout, g, wrh, wrl, br, tm, cap):
    n = x2.shape[0]
    ntt = n // tm
    row = lambda w: pl.BlockSpec((tm, w), lambda i: (jnp.minimum(i, ntt - 1), 0))
    full = lambda a: pl.BlockSpec(a.shape, lambda i: (0,) * a.ndim)
    return pl.pallas_call(
        _merge_kernel,
        grid=(ntt + 1,),
        in_specs=[row(ATTN_WIDTH), row(LRU_WIDTH), row(D_MODEL), row(D_MODEL), row(D_MODEL),
                  full(wao), full(wlo), full(wout), full(g), full(wrh), full(wrl), full(br)],
        out_specs=[row(D_MODEL), pl.BlockSpec((cap, HALF), lambda i: (i, 0)),
                   row(LANES),
                   pl.BlockSpec((N_EXPERTS, LANES), lambda i: (jnp.minimum(i, ntt - 1), 0))],
        out_shape=[jax.ShapeDtypeStruct((n, D_MODEL), F32),
                   jax.ShapeDtypeStruct(((ntt + 1) * cap, HALF), jnp.uint32),
                   jax.ShapeDtypeStruct((n, LANES), F32),
                   jax.ShapeDtypeStruct((ntt * N_EXPERTS, LANES), F32)],
        compiler_params=pltpu.CompilerParams(
            dimension_semantics=("arbitrary",), vmem_limit_bytes=VMEM_LIMIT),
        name="merge_router",
    )(o, hl, sga, sgl, x2, wao, wlo, wout, g, wrh, wrl, br)


def _expert_kernel(te_ref, na_ref, src_ref, dst_ref, hc_hbm_ref, wgu_ref, bgu_ref, wd_ref, bd_ref,
                   yc_ref, xb0, xb1, yb0, yb1, wgu_bf, wd_bf, sem_in, sem_out):
    del hc_hbm_ref
    i = pl.program_id(0)
    na = na_ref[0]
    xb = (xb0, xb1)
    yb = (yb0, yb1)
    n_chunks = xb0.shape[0] // CHUNK

    def hbm_rows(chunk):
        return yc_ref.at[pl.ds(pl.multiple_of(chunk * CHUNK, CHUNK), CHUNK)]

    def start_in(tile, s):
        for j in range(n_chunks):
            pltpu.make_async_copy(hbm_rows(src_ref[tile * n_chunks + j]),
                                  xb[s].at[pl.ds(j * CHUNK, CHUNK)], sem_in.at[s]).start()

    def start_out(tile, s):
        for j in range(n_chunks):
            pltpu.make_async_copy(yb[s].at[pl.ds(j * CHUNK, CHUNK)],
                                  hbm_rows(dst_ref[tile * n_chunks + j]), sem_out.at[s]).start()

    def wait_in(s):
        pltpu.make_async_copy(xb[s], xb[s], sem_in.at[s]).wait()

    def wait_out(s):
        pltpu.make_async_copy(yb[s], yb[s], sem_out.at[s]).wait()

    @pl.when(i == 0)
    def _():
        start_in(0, 0)

    @pl.when((i < na) & ((i == 0) | (te_ref[i] != te_ref[jnp.maximum(i - 1, 0)])))
    def _():
        rows = 128

        def cast_gu(r, c):
            sl = pl.ds(pl.multiple_of(r * rows, rows), rows)
            wgu_bf[sl, :] = wgu_ref[0, sl, :].astype(BF16)
            return c

        def cast_d(r, c):
            sl = pl.ds(pl.multiple_of(r * rows, rows), rows)
            wd_bf[sl, :] = wd_ref[0, sl, :].astype(BF16)
            return c

        lax.fori_loop(0, D_MODEL // rows, cast_gu, 0)
        lax.fori_loop(0, D_EXPERT // rows, cast_d, 0)

    def tile_step(s):
        @pl.when(i >= 2)
        def _():
            wait_out(s)

        @pl.when(i + 1 < na)
        def _():
            start_in(i + 1, 1 - s)

        wait_in(s)
        w = xb[s][...]
        lo = _unpack_row(w, 0).astype(BF16)
        hi = _unpack_row(w, 1).astype(BF16)
        gu = (jnp.dot(lo, wgu_bf[:HALF, :], preferred_element_type=F32)
              + jnp.dot(hi, wgu_bf[HALF:, :], preferred_element_type=F32)
              + bgu_ref[0])
        g = jnp.minimum(gu[:, :D_EXPERT], SWIGLU_LIMIT)
        u = jnp.clip(gu[:, D_EXPERT:], -SWIGLU_LIMIT, SWIGLU_LIMIT)
        act = (u + 1.0) * (g * jax.nn.sigmoid(SWIGLU_ALPHA * g))
        y = jnp.dot(act.astype(BF16), wd_bf[...], preferred_element_type=F32) + bd_ref[0]
        yb[s][...] = _pack_row(y)
        start_out(i, s)

        @pl.when(i == na - 1)
        def _():
            @pl.when(i >= 1)
            def _():
                wait_out(1 - s)

            wait_out(s)

    for s in range(2):
        pl.when((i < na) & (i % 2 == s))(functools.partial(tile_step, s))


def _experts(tile_e, n_active, chunk_src, chunk_dst, hc, wgu, bgu, wd, bd, tile_rows, n_tiles):
    per_e = lambda a: pl.BlockSpec((1,) + a.shape[1:], lambda i, te, na, src, dst: (te[i], 0, 0))
    tile_buf = pltpu.VMEM((tile_rows, HALF), jnp.uint32)
    return pl.pallas_call(
        _expert_kernel,
        grid_spec=pltpu.PrefetchScalarGridSpec(
            num_scalar_prefetch=4,
            grid=(n_tiles,),
            in_specs=[pl.BlockSpec(memory_space=pl.ANY),
                      per_e(wgu), per_e(bgu), per_e(wd), per_e(bd)],
            out_specs=pl.BlockSpec(memory_space=pl.ANY),
            scratch_shapes=[tile_buf, tile_buf, tile_buf, tile_buf,
                            pltpu.VMEM(wgu.shape[1:], BF16), pltpu.VMEM(wd.shape[1:], BF16),
                            pltpu.SemaphoreType.DMA((2,)),
                            pltpu.SemaphoreType.DMA((2,))],
        ),
        out_shape=jax.ShapeDtypeStruct(hc.shape, jnp.uint32),
        input_output_aliases={4: 0},
        compiler_params=pltpu.CompilerParams(
            dimension_semantics=("arbitrary",), vmem_limit_bytes=VMEM_LIMIT),
        name="experts",
    )(tile_e, n_active, chunk_src, chunk_dst, hc, wgu, bgu, wd, bd)


def _combine_kernel(yc_ref, meta_ref, x1_ref, g_ref, o_ref):
    tm = x1_ref.shape[0]
    cap = yc_ref.shape[0]
    meta = meta_ref[...]
    pos = meta[:, :TOP_K].astype(jnp.int32).astype(jnp.int16)
    gate = meta[:, TOP_K:2 * TOP_K].astype(BF16)
    slot_id = lax.broadcasted_iota(jnp.int16, (tm, cap), 1)
    sel = jnp.zeros((tm, cap), BF16)
    for k in range(TOP_K):
        sel = jnp.where(slot_id == pos[:, k:k + 1], gate[:, k:k + 1], sel)
    w = yc_ref[...]
    lo = jnp.dot(sel, _unpack_row(w, 0).astype(BF16), preferred_element_type=F32)
    hi = jnp.dot(sel, _unpack_row(w, 1).astype(BF16), preferred_element_type=F32)
    x2 = x1_ref[...] + jnp.concatenate([lo, hi], axis=-1)
    o_ref[...] = _rms(x2, g_ref[...])


def _combine(yc, meta, x1, g, tm, cap):
    n = x1.shape[0]
    row = lambda w: pl.BlockSpec((tm, w), lambda i: (i, 0))
    return pl.pallas_call(
        _combine_kernel,
        grid=(n // tm,),
        in_specs=[pl.BlockSpec((cap, HALF), lambda i: (i, 0)), row(LANES),
                  row(D_MODEL), pl.BlockSpec((1, D_MODEL), lambda i: (0, 0))],
        out_specs=row(D_MODEL),
        out_shape=jax.ShapeDtypeStruct((n, D_MODEL), F32),
        compiler_params=pltpu.CompilerParams(
            dimension_semantics=("parallel",), vmem_limit_bytes=VMEM_LIMIT),
        name="combine",
    )(yc, meta, x1, g)


def _block_diag(w):
    eye = jnp.eye(LRU_HEADS, dtype=w.dtype)
    return jnp.einsum("hij,hg->higj", w, eye).reshape(LRU_WIDTH, LRU_WIDTH)


def _pick(n, pref):
    t = pref
    while n % t:
        t //= 2
    return t


def _tile_plan(cnt, tm, cap, tile_rows, n_tiles):
    i32 = jnp.int32
    ntt = cnt.shape[0]
    cpt = tile_rows // CHUNK
    pcc = (cnt + CHUNK - 1) // CHUNK
    seg_base = jnp.arange(ntt, dtype=i32)[:, None] * (cap // CHUNK) + jnp.cumsum(pcc, axis=1) - pcc
    seg_end = jnp.cumsum(pcc, axis=0)
    seg_start = seg_end - pcc
    tot = seg_end[-1]
    ntile = (tot + cpt - 1) // cpt
    tile_end = jnp.cumsum(ntile)
    tile_start = tile_end - ntile
    n_active = tile_end[-1]
    ids = jnp.arange(n_tiles + 1, dtype=i32)
    live = ids < n_active
    tid = jnp.minimum(ids, n_active - 1)
    tile_e = jnp.sum((tile_end[None, :] <= tid[:, None]).astype(i32), axis=1)
    q = ((tid - tile_start[tile_e]) * cpt)[:, None] + jnp.arange(cpt, dtype=i32)[None, :]
    starts = seg_start.T[tile_e][:, None, :]
    ends = seg_end.T[tile_e][:, None, :]
    shift = (seg_base - seg_start).T[tile_e][:, None, :]
    inside = (starts <= q[:, :, None]) & (q[:, :, None] < ends)
    chunk = q + jnp.sum(jnp.where(inside, shift, 0), axis=2)
    valid = live[:, None] & (q < tot[tile_e][:, None])
    spare = ntt * (cap // CHUNK)
    slot_spare = spare + (ids % 2)[:, None] * cpt + jnp.arange(cpt, dtype=i32)[None, :]
    src = jnp.where(valid, chunk, spare + 2 * cpt)
    dst = jnp.where(valid, chunk, slot_spare)
    return (tile_e[:n_tiles].astype(i32), n_active.reshape(1).astype(i32),
            src.reshape(-1).astype(i32), dst.reshape(-1).astype(i32))


def _layer(x2, pos2, bsz, seq, norm_mix_g, w_in, conv_w, conv_b, w_rg, b_rg, w_ig, b_ig,
           lru_lambda, attn_sinks, w_attn_o, w_lru_o, w_out, norm_ffn_g, w_router, b_router,
           w_gate_up, b_gate_up, w_down, b_down, norm_out_g):
    n = x2.shape[0]
    tm = _pick(n, TOKEN_TILE)
    row = lambda a: a.reshape(1, -1).astype(F32)

    half = np.arange(0, HEAD_DIM, 2, dtype=np.float32) / HEAD_DIM
    inv_freq = (1.0 / (ROPE_THETA ** jnp.asarray(half, F32)))
    invf = jnp.tile(inv_freq, LANES // (HEAD_DIM // 2)).reshape(1, LANES)

    q, k, v, xr, gy, sga, sgl = _in_proj(x2, pos2, row(norm_mix_g), invf, w_in.astype(BF16), tm)
    o = _attention(attn_sinks.astype(F32), q, k, v, bsz, seq, _pick(seq, ATTN_TILE))

    wg = jnp.concatenate([_block_diag(w_rg), _block_diag(w_ig)], axis=1).astype(BF16)
    bg = jnp.concatenate([b_rg, b_ig]).reshape(1, -1).astype(F32)
    hl = _lru(xr, gy, conv_w.astype(F32), row(conv_b), wg, bg, row(lru_lambda),
              bsz, seq, _pick(seq, LRU_TILE))

    cap = tm * TOP_K + N_EXPERTS * CHUNK
    wr = jnp.zeros((D_MODEL, LANES), F32).at[:, :N_EXPERTS].set(w_router.astype(F32))
    wrh = wr.astype(BF16)
    wrl = (wr - wrh.astype(F32)).astype(BF16)
    br = jnp.full((1, LANES), NEG_INF, F32).at[0, :N_EXPERTS].set(b_router.astype(F32))
    x1, hc, meta, cnt = _merge(
        o, hl, sga, sgl, x2, w_attn_o.astype(BF16), w_lru_o.astype(BF16), w_out.astype(BF16),
        row(norm_ffn_g), wrh, wrl, br, tm, cap)

    ntt = n // tm
    cpt = EXPERT_TILE // CHUNK
    max_chunks = ntt * (tm * TOP_K // CHUNK + N_EXPERTS)
    n_tiles = -(-max_chunks // cpt) + N_EXPERTS
    counts = cnt.reshape(ntt, N_EXPERTS, LANES)[:, :, 0].astype(jnp.int32)
    tile_e, n_active, chunk_src, chunk_dst = _tile_plan(counts, tm, cap, EXPERT_TILE, n_tiles)
    yc = _experts(tile_e, n_active, chunk_src, chunk_dst, hc,
                  w_gate_up.astype(F32), b_gate_up.reshape(N_EXPERTS, 1, -1).astype(F32),
                  w_down.astype(F32), b_down.reshape(N_EXPERTS, 1, -1).astype(F32),
                  EXPERT_TILE, n_tiles)
    return _combine(yc, meta, x1, row(norm_out_g), tm, cap)


def kernel(x, positions, norm_mix_g, w_in, conv_w, conv_b, w_rg, b_rg, w_ig, b_ig, lru_lambda,
           attn_sinks, w_attn_o, w_lru_o, w_out, norm_ffn_g, w_router, b_router, w_gate_up,
           b_gate_up, w_down, b_down, norm_final_g):
    bsz, seq, d = x.shape
    depth = w_in.shape[0]
    assert depth == 1 and d == D_MODEL and seq % BLOCK == 0
    x2 = x.reshape(bsz * seq, d)
    pos2 = positions.reshape(bsz * seq, 1).astype(jnp.int32)
    out = _layer(x2, pos2, bsz, seq, norm_mix_g[0], w_in[0], conv_w[0], conv_b[0], w_rg[0],
                 b_rg[0], w_ig[0], b_ig[0], lru_lambda[0], attn_sinks[0], w_attn_o[0],
                 w_lru_o[0], w_out[0], norm_ffn_g[0], w_router[0], b_router[0], w_gate_up[0],
                 b_gate_up[0], w_down[0], b_down[0], norm_final_g)
    return out.reshape(bsz, seq, d)
```

```python
import functools

import numpy as np
import jax
import jax.numpy as jnp
from jax import lax
from jax.experimental import pallas as pl
from jax.experimental.pallas import tpu as pltpu

F32 = jnp.float32
BF16 = jnp.bfloat16

D_MODEL = 1024
HEAD_DIM = 64
N_Q_HEADS = 8
N_KV_HEADS = 2
GROUP = N_Q_HEADS // N_KV_HEADS
ATTN_WIDTH = N_Q_HEADS * HEAD_DIM
KV_WIDTH = N_KV_HEADS * HEAD_DIM
WINDOW = 128
BLOCK = 128
ROPE_THETA = 10000.0
LRU_WIDTH = D_MODEL // 2
LRU_HEADS = 8
LRU_BLOCK = LRU_WIDTH // LRU_HEADS
LRU_C = 8.0
CONV_WIDTH = 4
N_EXPERTS = 32
TOP_K = 4
D_EXPERT = D_MODEL
SWIGLU_LIMIT = 7.0
SWIGLU_ALPHA = 1.702
NORM_EPS = 1e-6
NEG_INF = -1e30
IN_SPLITS = (ATTN_WIDTH, KV_WIDTH, KV_WIDTH, LRU_WIDTH, LRU_WIDTH, D_MODEL, D_MODEL)
IN_WIDTH = sum(IN_SPLITS)

LANES = 128
SUBLANES = 8
HALF = D_MODEL // 2
VMEM_LIMIT = 56 * 1024 * 1024

TOKEN_TILE = 512
ATTN_TILE = 512
EXPERT_TILE = 512
CHUNK = 8
PERM_SPLIT = 4

def _rms(x, g):
    return x * lax.rsqrt(jnp.mean(x * x, axis=-1, keepdims=True) + NORM_EPS) * g


def _sigmoid(z):
    return 0.5 * jnp.tanh(0.5 * z) + 0.5


def _pack_row(v):
    return pltpu.pack_elementwise([v[:, :HALF], v[:, HALF:]], packed_dtype=BF16)


def _unpack_row(w, index):
    return pltpu.unpack_elementwise(w, index=index, packed_dtype=BF16, unpacked_dtype=F32)


def _in_proj_kernel(tiles_per_seq, x_ref, pos_ref, g_ref, invf_ref, w_ref,
                    cw_ref, cb_ref, wg_ref, bg_ref, lam_ref,
                    q_ref, k_ref, v_ref, hl_ref, sga_ref, sgl_ref, tail_ref, h_ref):
    h = _rms(x_ref[...], g_ref[...]).astype(BF16)
    offs = np.cumsum((0,) + IN_SPLITS)

    def proj(piece):
        lo, hi = int(offs[piece]), int(offs[piece + 1])
        return jnp.dot(h, w_ref[:, lo:hi], preferred_element_type=F32)

    @pl.when(pl.program_id(0) % tiles_per_seq == 0)
    def _():
        tail_ref[...] = jnp.zeros_like(tail_ref)
        h_ref[...] = jnp.zeros_like(h_ref)

    hl = _lru_tile(proj(3), cw_ref[...], cb_ref[...], wg_ref, bg_ref[...], lam_ref[...],
                   tail_ref, h_ref)
    hl_ref[...] = (hl * jax.nn.gelu(proj(4))).astype(BF16)

    ang = pos_ref[...].astype(F32) * invf_ref[...]
    cos = jnp.cos(ang)
    sin = jnp.sin(ang)
    lane = lax.broadcasted_iota(jnp.int32, (1, LANES), 1)
    first_half = (lane % HEAD_DIM) < (HEAD_DIM // 2)
    sin_signed = jnp.where(first_half, -sin, sin)

    def rope(t):
        width = t.shape[-1]
        reps = width // LANES
        c = jnp.concatenate([cos] * reps, axis=-1) if reps > 1 else cos
        s = jnp.concatenate([sin_signed] * reps, axis=-1) if reps > 1 else sin_signed
        fh = jnp.concatenate([first_half] * reps, axis=-1) if reps > 1 else first_half
        nxt = pltpu.roll(t, width - HEAD_DIM // 2, axis=1)
        prv = pltpu.roll(t, HEAD_DIM // 2, axis=1)
        return t * c + jnp.where(fh, nxt, prv) * s

    q_ref[...] = (rope(proj(0)) * (HEAD_DIM ** -0.5)).astype(BF16)
    k_ref[...] = rope(proj(1)).astype(BF16)
    v_ref[...] = proj(2).astype(BF16)
    sga_ref[...] = _sigmoid(proj(5)).astype(BF16)
    sgl_ref[...] = _sigmoid(proj(6)).astype(BF16)


def _lru_tile(x, cw, cb, wg_ref, bg, lam, tail_ref, h_ref):
    ts = x.shape[0]
    prev = tail_ref[...]
    row8 = lax.broadcasted_iota(jnp.int32, (8, LRU_WIDTH), 0)
    xc = x * cw[CONV_WIDTH - 1:CONV_WIDTH] + cb
    for d in range(1, CONV_WIDTH):
        xs = pltpu.roll(x, d, axis=0)
        ps = pltpu.roll(prev, d, axis=0)
        head = jnp.where(row8 < d, ps, xs[0:8])
        xs = jnp.concatenate([head, xs[8:]], axis=0)
        xc = xc + xs * cw[CONV_WIDTH - 1 - d:CONV_WIDTH - d]
    tail_ref[...] = x[ts - 8:ts]

    gates = jnp.dot(xc.astype(BF16), wg_ref[...], preferred_element_type=F32) + bg
    r = _sigmoid(gates[:, :LRU_WIDTH])
    i = _sigmoid(gates[:, LRU_WIDTH:])
    log_a = (-LRU_C) * r * jax.nn.softplus(-lam)
    a = jnp.exp(log_a)
    y = 1.0 - a * a
    u = jnp.where(y > 0.0, y * lax.rsqrt(y), 0.0) * (i * xc)

    groups = ts // SUBLANES
    a = a.reshape(groups, SUBLANES, LRU_WIDTH)
    u = u.reshape(groups, SUBLANES, LRU_WIDTH)
    sub = lax.broadcasted_iota(jnp.int32, (groups, SUBLANES, LRU_WIDTH), 1)
    d = 1
    while d < SUBLANES:
        keep = sub >= d
        a_s = jnp.where(keep, pltpu.roll(a, d, axis=1), 1.0)
        u_s = jnp.where(keep, pltpu.roll(u, d, axis=1), 0.0)
        u = a * u_s + u
        a = a * a_s
        d *= 2
    carry = h_ref[0:1]
    hs = []
    for g in range(groups):
        hg = a[g] * carry + u[g]
        hs.append(hg)
        carry = hg[SUBLANES - 1:SUBLANES]
    h_ref[...] = jnp.broadcast_to(carry, h_ref.shape)
    return jnp.concatenate(hs, axis=0)


def _in_proj(x2, pos2, g, invf, w_in_bf, cw, cb, wg, bg, lam, tm, seq):
    n = x2.shape[0]
    row = lambda w: pl.BlockSpec((tm, w), lambda i: (i, 0))
    full = lambda a: pl.BlockSpec(a.shape, lambda i: (0,) * a.ndim)
    widths = (ATTN_WIDTH, KV_WIDTH, KV_WIDTH, LRU_WIDTH, D_MODEL, D_MODEL)
    return pl.pallas_call(
        functools.partial(_in_proj_kernel, seq // tm),
        grid=(n // tm,),
        in_specs=[row(D_MODEL), row(1), full(g), full(invf), full(w_in_bf),
                  full(cw), full(cb), full(wg), full(bg), full(lam)],
        out_specs=[row(w) for w in widths],
        out_shape=[jax.ShapeDtypeStruct((n, w), BF16) for w in widths],
        scratch_shapes=[pltpu.VMEM((8, LRU_WIDTH), F32), pltpu.VMEM((8, LRU_WIDTH), F32)],
        compiler_params=pltpu.CompilerParams(
            dimension_semantics=("arbitrary",), vmem_limit_bytes=VMEM_LIMIT),
        name="in_proj",
    )(x2, pos2, g, invf, w_in_bf, cw, cb, wg, bg, lam)


def _attn_kernel(sink_ref, q_ref, kc_ref, kp_ref, vc_ref, vp_ref, o_ref):
    nblk = q_ref.shape[0] // BLOCK
    j = pl.program_id(1)
    row = lax.broadcasted_iota(jnp.int32, (BLOCK, BLOCK), 0)
    col = lax.broadcasted_iota(jnp.int32, (BLOCK, BLOCK), 1)
    own = col <= row
    ones = jnp.ones((BLOCK, HEAD_DIM), BF16)
    dims = (((1,), (1,)), ((), ()))
    for i in range(nblk):
        rows = slice(i * BLOCK, (i + 1) * BLOCK)
        q = q_ref[rows, :]
        if i == 0:
            k_prev, v_prev = kp_ref[...], vp_ref[...]
            no_prev = jnp.where(j > 0, 0.0, NEG_INF)
        else:
            before = slice((i - 1) * BLOCK, i * BLOCK)
            k_prev, v_prev = kc_ref[before, :], vc_ref[before, :]
            no_prev = None
        k_own, v_own = kc_ref[rows, :], vc_ref[rows, :]
        heads = [slice((h // GROUP) * HEAD_DIM, (h // GROUP + 1) * HEAD_DIM)
                 for h in range(N_Q_HEADS)]
        scores = []
        for h in range(N_Q_HEADS):
            qh = q[:, h * HEAD_DIM:(h + 1) * HEAD_DIM]
            s_own = lax.dot_general(qh, k_own[:, heads[h]], dims, preferred_element_type=F32)
            s_prev = lax.dot_general(qh, k_prev[:, heads[h]], dims, preferred_element_type=F32)
            if no_prev is not None:
                s_prev = s_prev + no_prev
            scores.append(jnp.where(own, s_own, s_prev))
        probs, tails = [], []
        for h in range(N_Q_HEADS):
            sink = sink_ref[h]
            m = jnp.maximum(jnp.max(scores[h], axis=-1, keepdims=True), sink)
            p = jnp.exp(scores[h] - m)
            probs.append((jnp.where(own, p, 0.0).astype(BF16), jnp.where(own, 0.0, p).astype(BF16)))
            tails.append(jnp.exp(sink - m))
        v_ext = [(jnp.concatenate([v_own[:, heads[g * GROUP]], ones], axis=-1),
                  jnp.concatenate([v_prev[:, heads[g * GROUP]], ones], axis=-1))
                 for g in range(N_KV_HEADS)]
        outs = []
        for h in range(N_Q_HEADS):
            vo, vp = v_ext[h // GROUP]
            acc = (jnp.dot(probs[h][0], vo, preferred_element_type=F32)
                   + jnp.dot(probs[h][1], vp, preferred_element_type=F32))
            denom = acc[:, HEAD_DIM:HEAD_DIM + 1] + tails[h]
            outs.append(acc[:, :HEAD_DIM] / denom)
        o_ref[rows, :] = jnp.concatenate(outs, axis=-1).astype(BF16)


def _attention(sinks, q, k, v, bsz, seq, tq):
    n = q.shape[0]
    nt = seq // tq
    per = tq // BLOCK
    cur = lambda w: pl.BlockSpec((tq, w), lambda b, j: (b * nt + j, 0))
    prev = lambda w: pl.BlockSpec(
        (BLOCK, w), lambda b, j: (b * nt * per + jnp.maximum(j * per - 1, 0), 0))
    return pl.pallas_call(
        _attn_kernel,
        grid=(bsz, nt),
        in_specs=[pl.BlockSpec(memory_space=pltpu.SMEM),
                  cur(ATTN_WIDTH), cur(KV_WIDTH), prev(KV_WIDTH), cur(KV_WIDTH), prev(KV_WIDTH)],
        out_specs=cur(ATTN_WIDTH),
        out_shape=jax.ShapeDtypeStruct((n, ATTN_WIDTH), BF16),
        compiler_params=pltpu.CompilerParams(
            dimension_semantics=("parallel", "parallel"), vmem_limit_bytes=VMEM_LIMIT),
        name="attn",
    )(sinks, q, k, k, v, v)


def _merge_kernel(o_ref, hl_ref, sga_ref, sgl_ref, x_ref, wao_ref, wlo_ref, wout_ref,
                  g_ref, wrh_ref, wrl_ref, br_ref,
                  x1_ref, hc_ref, meta_ref, cnt_ref):
    last = pl.num_programs(0) - 1

    @pl.when(pl.program_id(0) == last)
    def _():
        hc_ref[...] = jnp.zeros_like(hc_ref)

    @pl.when(pl.program_id(0) < last)
    def _():
        _merge_tile(o_ref, hl_ref, sga_ref, sgl_ref, x_ref, wao_ref, wlo_ref, wout_ref,
                    g_ref, wrh_ref, wrl_ref, br_ref, x1_ref, hc_ref, meta_ref, cnt_ref)


def _merge_tile(o_ref, hl_ref, sga_ref, sgl_ref, x_ref, wao_ref, wlo_ref, wout_ref,
                g_ref, wrh_ref, wrl_ref, br_ref,
                x1_ref, hc_ref, meta_ref, cnt_ref):
    tm = x_ref.shape[0]
    cap = hc_ref.shape[0]

    ya = jnp.dot(o_ref[...], wao_ref[...], preferred_element_type=F32)
    yl = jnp.dot(hl_ref[...], wlo_ref[...], preferred_element_type=F32)
    merged = sga_ref[...].astype(F32) * ya + sgl_ref[...].astype(F32) * yl
    x1 = x_ref[...] + jnp.dot(merged.astype(BF16), wout_ref[...], preferred_element_type=F32)
    x1_ref[...] = x1
    h2 = _rms(x1, g_ref[...])
    h2b = h2.astype(BF16)

    h2l = (h2 - h2b.astype(F32)).astype(BF16)
    logits = (jnp.dot(h2b, wrh_ref[...], preferred_element_type=F32)
              + jnp.dot(h2l, wrh_ref[...], preferred_element_type=F32)
              + jnp.dot(h2b, wrl_ref[...], preferred_element_type=F32)
              + br_ref[...])
    lt = jnp.transpose(logits)[:N_EXPERTS]
    e_id = lax.broadcasted_iota(jnp.int32, (N_EXPERTS, tm), 0)
    work = lt
    vals, sels = [], []
    for _ in range(TOP_K):
        m = jnp.max(work, axis=0, keepdims=True)
        sel = jnp.min(jnp.where(work == m, e_id, N_EXPERTS), axis=0, keepdims=True)
        vals.append(m)
        sels.append(sel)
        work = jnp.where(e_id == sel, -jnp.inf, work)
    exps = [jnp.exp(v - vals[0]) for v in vals]
    tot = exps[0] + exps[1] + exps[2] + exps[3]

    onehot = jnp.zeros((N_EXPERTS, tm), F32)
    for sel in sels:
        onehot = onehot + (e_id == sel).astype(F32)
    s_i = lax.broadcasted_iota(jnp.int32, (tm, tm), 0)
    t_i = lax.broadcasted_iota(jnp.int32, (tm, tm), 1)
    earlier = jnp.dot(onehot.astype(BF16), (s_i < t_i).astype(BF16),
                      preferred_element_type=F32)
    cnt = jnp.sum(onehot, axis=1, keepdims=True)
    chunks = jnp.floor((cnt + (CHUNK - 1)) * (1.0 / CHUNK))
    e_r = lax.broadcasted_iota(jnp.int32, (N_EXPERTS, N_EXPERTS), 0)
    e_c = lax.broadcasted_iota(jnp.int32, (N_EXPERTS, N_EXPERTS), 1)
    start = jnp.dot((e_c < e_r).astype(BF16),
                    jnp.broadcast_to(chunks, (N_EXPERTS, LANES)).astype(BF16),
                    preferred_element_type=F32)[:, 0:1] * float(CHUNK)
    slot_of = earlier + start

    pos_rows = [jnp.sum(jnp.where(e_id == sels[k], slot_of, 0.0), axis=0, keepdims=True)
                for k in range(TOP_K)]
    r_id = lax.broadcasted_iota(jnp.int32, (2 * TOP_K, tm), 0)
    meta = jnp.zeros((2 * TOP_K, tm), F32)
    for k in range(TOP_K):
        meta = jnp.where(r_id == k, pos_rows[k], meta)
        meta = jnp.where(r_id == TOP_K + k, exps[k] / tot, meta)
    meta = jnp.concatenate([meta, jnp.zeros((LANES - 2 * TOP_K, tm), F32)], axis=0)
    meta_ref[...] = jnp.transpose(meta)
    cnt_ref[...] = jnp.broadcast_to(cnt, cnt_ref.shape)

    pos_t = [p.astype(jnp.int32).astype(jnp.int16) for p in pos_rows]
    rows_per = cap // PERM_SPLIT
    for c in range(PERM_SPLIT):
        slot_id = lax.broadcasted_iota(jnp.int16, (rows_per, tm), 0) + c * rows_per
        perm = jnp.zeros((rows_per, tm), BF16)
        for k in range(TOP_K):
            perm = jnp.where(slot_id == pos_t[k], 1.0, perm).astype(BF16)
        rows = jnp.dot(perm, h2b, preferred_element_type=F32)
        hc_ref[c * rows_per:(c + 1) * rows_per, :] = _pack_row(rows)


def _merge(o, hl, sga, sgl, x2, wao, wlo, wout, g, wrh, wrl, br, tm, cap):
    n = x2.shape[0]
    ntt = n // tm
    row = lambda w: pl.BlockSpec((tm, w), lambda i: (jnp.minimum(i, ntt - 1), 0))
    full = lambda a: pl.BlockSpec(a.shape, lambda i: (0,) * a.ndim)
    return pl.pallas_call(
        _merge_kernel,
        grid=(ntt + 1,),
        in_specs=[row(ATTN_WIDTH), row(LRU_WIDTH), row(D_MODEL), row(D_MODEL), row(D_MODEL),
                  full(wao), full(wlo), full(wout), full(g), full(wrh), full(wrl), full(br)],
        out_specs=[row(D_MODEL), pl.BlockSpec((cap, HALF), lambda i: (i, 0)),
                   row(LANES),
                   pl.BlockSpec((N_EXPERTS, LANES), lambda i: (jnp.minimum(i, ntt - 1), 0))],
        out_shape=[jax.ShapeDtypeStruct((n, D_MODEL), F32),
                   jax.ShapeDtypeStruct(((ntt + 1) * cap, HALF), jnp.uint32),
                   jax.ShapeDtypeStruct((n, LANES), F32),
                   jax.ShapeDtypeStruct((ntt * N_EXPERTS, LANES), F32)],
        compiler_params=pltpu.CompilerParams(
            dimension_semantics=("arbitrary",), vmem_limit_bytes=VMEM_LIMIT),
        name="merge_router",
    )(o, hl, sga, sgl, x2, wao, wlo, wout, g, wrh, wrl, br)


def _expert_kernel(te_ref, na_ref, src_ref, dst_ref, hc_hbm_ref, wgu_ref, bgu_ref, wd_ref, bd_ref,
                   yc_ref, xb0, xb1, yb0, yb1, wgu_bf, wd_bf, sem_in, sem_out):
    del hc_hbm_ref
    i = pl.program_id(0)
    na = na_ref[0]
    xb = (xb0, xb1)
    yb = (yb0, yb1)
    n_chunks = xb0.shape[0] // CHUNK

    def hbm_rows(chunk):
        return yc_ref.at[pl.ds(pl.multiple_of(chunk * CHUNK, CHUNK), CHUNK)]

    def start_in(tile, s):
        for j in range(n_chunks):
            pltpu.make_async_copy(hbm_rows(src_ref[tile * n_chunks + j]),
                                  xb[s].at[pl.ds(j * CHUNK, CHUNK)], sem_in.at[s]).start()

    def start_out(tile, s):
        for j in range(n_chunks):
            pltpu.make_async_copy(yb[s].at[pl.ds(j * CHUNK, CHUNK)],
                                  hbm_rows(dst_ref[tile * n_chunks + j]), sem_out.at[s]).start()

    def wait_in(s):
        pltpu.make_async_copy(xb[s], xb[s], sem_in.at[s]).wait()

    def wait_out(s):
        pltpu.make_async_copy(yb[s], yb[s], sem_out.at[s]).wait()

    @pl.when(i == 0)
    def _():
        start_in(0, 0)

    @pl.when((i < na) & ((i == 0) | (te_ref[i] != te_ref[jnp.maximum(i - 1, 0)])))
    def _():
        rows = 128

        def cast_gu(r, c):
            sl = pl.ds(pl.multiple_of(r * rows, rows), rows)
            wgu_bf[sl, :] = wgu_ref[0, sl, :].astype(BF16)
            return c

        def cast_d(r, c):
            sl = pl.ds(pl.multiple_of(r * rows, rows), rows)
            wd_bf[sl, :] = wd_ref[0, sl, :].astype(BF16)
            return c

        lax.fori_loop(0, D_MODEL // rows, cast_gu, 0)
        lax.fori_loop(0, D_EXPERT // rows, cast_d, 0)

    def tile_step(s):
        @pl.when(i >= 2)
        def _():
            wait_out(s)

        @pl.when(i + 1 < na)
        def _():
            start_in(i + 1, 1 - s)

        wait_in(s)
        w = xb[s][...]
        lo = _unpack_row(w, 0).astype(BF16)
        hi = _unpack_row(w, 1).astype(BF16)
        gu = (jnp.dot(lo, wgu_bf[:HALF, :], preferred_element_type=F32)
              + jnp.dot(hi, wgu_bf[HALF:, :], preferred_element_type=F32)
              + bgu_ref[0])
        g = jnp.minimum(gu[:, :D_EXPERT], SWIGLU_LIMIT)
        u = jnp.clip(gu[:, D_EXPERT:], -SWIGLU_LIMIT, SWIGLU_LIMIT)
        act = (u + 1.0) * (g * jax.nn.sigmoid(SWIGLU_ALPHA * g))
        y = jnp.dot(act.astype(BF16), wd_bf[...], preferred_element_type=F32) + bd_ref[0]
        yb[s][...] = _pack_row(y)
        start_out(i, s)

        @pl.when(i == na - 1)
        def _():
            @pl.when(i >= 1)
            def _():
                wait_out(1 - s)

            wait_out(s)

    for s in range(2):
        pl.when((i < na) & (i % 2 == s))(functools.partial(tile_step, s))


def _experts(tile_e, n_active, chunk_src, chunk_dst, hc, wgu, bgu, wd, bd, tile_rows, n_tiles):
    per_e = lambda a: pl.BlockSpec((1,) + a.shape[1:], lambda i, te, na, src, dst: (te[i], 0, 0))
    tile_buf = pltpu.VMEM((tile_rows, HALF), jnp.uint32)
    return pl.pallas_call(
        _expert_kernel,
        grid_spec=pltpu.PrefetchScalarGridSpec(
            num_scalar_prefetch=4,
            grid=(n_tiles,),
            in_specs=[pl.BlockSpec(memory_space=pl.ANY),
                      per_e(wgu), per_e(bgu), per_e(wd), per_e(bd)],
            out_specs=pl.BlockSpec(memory_space=pl.ANY),
            scratch_shapes=[tile_buf, tile_buf, tile_buf, tile_buf,
                            pltpu.VMEM(wgu.shape[1:], BF16), pltpu.VMEM(wd.shape[1:], BF16),
                            pltpu.SemaphoreType.DMA((2,)),
                            pltpu.SemaphoreType.DMA((2,))],
        ),
        out_shape=jax.ShapeDtypeStruct(hc.shape, jnp.uint32),
        input_output_aliases={4: 0},
        compiler_params=pltpu.CompilerParams(
            dimension_semantics=("arbitrary",), vmem_limit_bytes=VMEM_LIMIT),
        name="experts",
    )(tile_e, n_active, chunk_src, chunk_dst, hc, wgu, bgu, wd, bd)


def _combine_kernel(yc_ref, meta_ref, x1_ref, g_ref, o_ref):
    tm = x1_ref.shape[0]
    cap = yc_ref.shape[0]
    meta = meta_ref[...]
    pos = meta[:, :TOP_K].astype(jnp.int32).astype(jnp.int16)
    gate = meta[:, TOP_K:2 * TOP_K].astype(BF16)
    slot_id = lax.broadcasted_iota(jnp.int16, (tm, cap), 1)
    sel = jnp.zeros((tm, cap), BF16)
    for k in range(TOP_K):
        sel = jnp.where(slot_id == pos[:, k:k + 1], gate[:, k:k + 1], sel)
    w = yc_ref[...]
    lo = jnp.dot(sel, _unpack_row(w, 0).astype(BF16), preferred_element_type=F32)
    hi = jnp.dot(sel, _unpack_row(w, 1).astype(BF16), preferred_element_type=F32)
    x2 = x1_ref[...] + jnp.concatenate([lo, hi], axis=-1)
    o_ref[...] = _rms(x2, g_ref[...])


def _combine(yc, meta, x1, g, tm, cap):
    n = x1.shape[0]
    row = lambda w: pl.BlockSpec((tm, w), lambda i: (i, 0))
    return pl.pallas_call(
        _combine_kernel,
        grid=(n // tm,),
        in_specs=[pl.BlockSpec((cap, HALF), lambda i: (i, 0)), row(LANES),
                  row(D_MODEL), pl.BlockSpec((1, D_MODEL), lambda i: (0, 0))],
        out_specs=row(D_MODEL),
        out_shape=jax.ShapeDtypeStruct((n, D_MODEL), F32),
        compiler_params=pltpu.CompilerParams(
            dimension_semantics=("parallel",), vmem_limit_bytes=VMEM_LIMIT),
        name="combine",
    )(yc, meta, x1, g)


def _block_diag(w):
    eye = jnp.eye(LRU_HEADS, dtype=w.dtype)
    return jnp.einsum("hij,hg->higj", w, eye).reshape(LRU_WIDTH, LRU_WIDTH)


def _pick(n, pref):
    t = pref
    while n % t:
        t //= 2
    return t


def _tile_plan(cnt, tm, cap, tile_rows, n_tiles):
    i32 = jnp.int32
    ntt = cnt.shape[0]
    cpt = tile_rows // CHUNK
    pcc = (cnt + CHUNK - 1) // CHUNK
    seg_base = jnp.arange(ntt, dtype=i32)[:, None] * (cap // CHUNK) + jnp.cumsum(pcc, axis=1) - pcc
    seg_end = jnp.cumsum(pcc, axis=0)
    seg_start = seg_end - pcc
    tot = seg_end[-1]
    ntile = (tot + cpt - 1) // cpt
    tile_end = jnp.cumsum(ntile)
    tile_start = tile_end - ntile
    n_active = tile_end[-1]
    ids = jnp.arange(n_tiles + 1, dtype=i32)
    live = ids < n_active
    tid = jnp.minimum(ids, n_active - 1)
    tile_e = jnp.sum((tile_end[None, :] <= tid[:, None]).astype(i32), axis=1)
    q = ((tid - tile_start[tile_e]) * cpt)[:, None] + jnp.arange(cpt, dtype=i32)[None, :]
    starts = seg_start.T[tile_e][:, :, None]
    ends = seg_end.T[tile_e][:, :, None]
    shift = (seg_base - seg_start).T[tile_e][:, :, None]
    inside = (starts <= q[:, None, :]) & (q[:, None, :] < ends)
    chunk = q + jnp.sum(jnp.where(inside, shift, 0), axis=1)
    valid = live[:, None] & (q < tot[tile_e][:, None])
    spare = ntt * (cap // CHUNK)
    slot_spare = spare + (ids % 2)[:, None] * cpt + jnp.arange(cpt, dtype=i32)[None, :]
    src = jnp.where(valid, chunk, spare + 2 * cpt)
    dst = jnp.where(valid, chunk, slot_spare)
    return (tile_e[:n_tiles].astype(i32), n_active.reshape(1).astype(i32),
            src.reshape(-1).astype(i32), dst.reshape(-1).astype(i32))


def _layer(x2, pos2, bsz, seq, norm_mix_g, w_in, conv_w, conv_b, w_rg, b_rg, w_ig, b_ig,
           lru_lambda, attn_sinks, w_attn_o, w_lru_o, w_out, norm_ffn_g, w_router, b_router,
           w_gate_up, b_gate_up, w_down, b_down, norm_out_g):
    n = x2.shape[0]
    tm = _pick(n, TOKEN_TILE)
    row = lambda a: a.reshape(1, -1).astype(F32)

    half = np.arange(0, HEAD_DIM, 2, dtype=np.float32) / HEAD_DIM
    inv_freq = (1.0 / (ROPE_THETA ** jnp.asarray(half, F32)))
    invf = jnp.tile(inv_freq, LANES // (HEAD_DIM // 2)).reshape(1, LANES)

    wg = jnp.concatenate([_block_diag(w_rg), _block_diag(w_ig)], axis=1).astype(BF16)
    bg = jnp.concatenate([b_rg, b_ig]).reshape(1, -1).astype(F32)
    q, k, v, hl, sga, sgl = _in_proj(x2, pos2, row(norm_mix_g), invf, w_in.astype(BF16),
                                     conv_w.astype(F32), row(conv_b), wg, bg, row(lru_lambda),
                                     tm, seq)
    o = _attention(attn_sinks.astype(F32), q, k, v, bsz, seq, _pick(seq, ATTN_TILE))

    cap = tm * TOP_K + N_EXPERTS * CHUNK
    wr = jnp.zeros((D_MODEL, LANES), F32).at[:, :N_EXPERTS].set(w_router.astype(F32))
    wrh = wr.astype(BF16)
    wrl = (wr - wrh.astype(F32)).astype(BF16)
    br = jnp.full((1, LANES), NEG_INF, F32).at[0, :N_EXPERTS].set(b_router.astype(F32))
    x1, hc, meta, cnt = _merge(
        o, hl, sga, sgl, x2, w_attn_o.astype(BF16), w_lru_o.astype(BF16), w_out.astype(BF16),
        row(norm_ffn_g), wrh, wrl, br, tm, cap)

    ntt = n // tm
    cpt = EXPERT_TILE // CHUNK
    max_chunks = ntt * (tm * TOP_K // CHUNK + N_EXPERTS)
    n_tiles = -(-max_chunks // cpt) + N_EXPERTS
    counts = cnt.reshape(ntt, N_EXPERTS, LANES)[:, :, 0].astype(jnp.int32)
    tile_e, n_active, chunk_src, chunk_dst = _tile_plan(counts, tm, cap, EXPERT_TILE, n_tiles)
    yc = _experts(tile_e, n_active, chunk_src, chunk_dst, hc,
                  w_gate_up.astype(F32), b_gate_up.reshape(N_EXPERTS, 1, -1).astype(F32),
                  w_down.astype(F32), b_down.reshape(N_EXPERTS, 1, -1).astype(F32),
                  EXPERT_TILE, n_tiles)
    return _combine(yc, meta, x1, row(norm_out_g), tm, cap)


def kernel(x, positions, norm_mix_g, w_in, conv_w, conv_b, w_rg, b_rg, w_ig, b_ig, lru_lambda,
           attn_sinks, w_attn_o, w_lru_o, w_out, norm_ffn_g, w_router, b_router, w_gate_up,
           b_gate_up, w_down, b_down, norm_final_g):
    bsz, seq, d = x.shape
    depth = w_in.shape[0]
    assert depth == 1 and d == D_MODEL and seq % BLOCK == 0
    x2 = x.reshape(bsz * seq, d)
    pos2 = positions.reshape(bsz * seq, 1).astype(jnp.int32)
    out = _layer(x2, pos2, bsz, seq, norm_mix_g[0], w_in[0], conv_w[0], conv_b[0], w_rg[0],
                 b_rg[0], w_ig[0], b_ig[0], lru_lambda[0], attn_sinks[0], w_attn_o[0],
                 w_lru_o[0], w_out[0], norm_ffn_g[0], w_router[0], b_router[0], w_gate_up[0],
                 b_gate_up[0], w_down[0], b_down[0], norm_final_g)
    return out.reshape(bsz, seq, d)
```
